```python
import math
import jax
import jax.numpy as jnp
from jax import lax
import numpy as np

D_MODEL = 2048
BATCH = 32
SEQ = 256
DEPTH = 2
DEC_BATCH = 8
DEC_SEQ = 4096
PAST_LEN = 512

GRID_W = 64
EPS = 1e-6
CHUNK = 64
Q_BLOCK = 128
H_A = 4
DK_A = 128
DV_A = 128
W_A = H_A * DV_A
CONV_K = 5
H_B = 8
Q_LORA = 512
KV_LORA = 256
NOPE_DIM = 128
ROPE_DIM = 64
AXIS_ROPE = ROPE_DIM // 2
DV_B = 128
W_B = H_B * DV_B
ROPE_THETA = 10000.0
SM_SCALE = (NOPE_DIM + ROPE_DIM) ** -0.5
H_C = 4
DK_C = 128
DV_C = 128
W_C = H_C * DV_C
MIX_W = W_A + W_B + W_C
SPLIT_SIZES = (H_A * DK_A, H_A * DK_A, W_A, W_A, 2 * H_A, 2 * H_A,
               Q_LORA, KV_LORA, ROPE_DIM, W_B,
               H_C * DK_C, H_C * DK_C, W_C, W_C, W_C, 2 * H_C, 2 * H_C)
N_IN = sum(SPLIT_SIZES)

kernel_name = 'hybrid_gdn_mla_mlstm_diffusion_step'


def _rmsnorm(x, g):
    xf = x.astype(jnp.float32)
    y = xf * lax.rsqrt(jnp.mean(xf * xf, axis=-1, keepdims=True) + EPS)
    return (y * g.astype(jnp.float32)).astype(x.dtype)


def _l2norm(x):
    return x * lax.rsqrt(jnp.sum(x * x, axis=-1, keepdims=True) + EPS)


def _heads(x, h):
    b, l, _ = x.shape
    return x.reshape(b, l, h, -1).transpose(0, 2, 1, 3)


def _unheads(x):
    b, h, l, d = x.shape
    return x.transpose(0, 2, 1, 3).reshape(b, l, h * d)


def _flip(t):
    return jnp.flip(t, axis=2)


def _centred_dwconv(x, w):
    k = w.shape[0]
    return lax.conv_general_dilated(x, w[:, None, :].astype(x.dtype), (1,), [(k // 2, k // 2)],
                                    dimension_numbers=('NWC', 'WIO', 'NWC'),
                                    feature_group_count=x.shape[-1])


def _axial_angles(rows):
    row = jnp.repeat(jnp.arange(rows, dtype=jnp.float32), GRID_W)
    col = jnp.tile(jnp.arange(GRID_W, dtype=jnp.float32), rows)
    inv = ROPE_THETA ** (-jnp.arange(0, AXIS_ROPE, 2, dtype=jnp.float32) / AXIS_ROPE)
    return row[:, None] * inv, col[:, None] * inv


def _rope_half(x, ang):
    x1, x2 = jnp.split(x, 2, axis=-1)
    cos = jnp.cos(ang).astype(x.dtype)
    sin = jnp.sin(ang).astype(x.dtype)
    return jnp.concatenate([x1 * cos - x2 * sin, x1 * sin + x2 * cos], axis=-1)


def _axial_rope(x, ang_row, ang_col):
    xr, xc = jnp.split(x, 2, axis=-1)
    return jnp.concatenate([_rope_half(xr, ang_row), _rope_half(xc, ang_col)], axis=-1)


def _modulate(x, cond, p):
    mod = jax.nn.silu(cond) @ p['w_ada'] + p['b_ada']
    shift, scale, gate = jnp.split(mod[:, None, :], 3, axis=-1)
    h = _rmsnorm(x, p['g_pre']) * (1 + scale) + shift
    return h, gate


def _split_proj(h, w_in):
    points = np.cumsum(SPLIT_SIZES)[:-1].tolist()
    return jnp.split(h @ w_in, points, axis=-1)


def _gdn_chunked(q, k, v, g, beta, s0):
    b, h, l, dk = q.shape
    n = l // CHUNK

    def chunks(t):
        return t.reshape(b, h, n, CHUNK, *t.shape[3:])

    q = chunks(q * dk ** -0.5)
    k = chunks(k)
    v = chunks(v)
    beta = chunks(beta)
    g = jnp.cumsum(chunks(g), axis=-1)
    incl = jnp.tril(jnp.ones((CHUNK, CHUNK), dtype=bool))
    strict = jnp.tril(jnp.ones((CHUNK, CHUNK), dtype=bool), -1)
    decay = jnp.exp(jnp.where(incl, g[..., :, None] - g[..., None, :], -jnp.inf))
    kb = k * beta[..., None]
    vb = v * beta[..., None]
    a = jnp.where(strict, jnp.einsum('bhncd,bhnsd->bhncs', kb, k) * decay, 0.0)
    eye = jnp.eye(CHUNK, dtype=a.dtype)
    t = lax.linalg.triangular_solve(a + eye, jnp.broadcast_to(eye, a.shape),
                                    left_side=True, lower=True, unit_diagonal=True)
    u = jnp.einsum('bhncs,bhnsv->bhncv', t, vb)
    w = jnp.einsum('bhncs,bhnsk->bhnck', t, kb * jnp.exp(g)[..., None])
    qk = jnp.einsum('bhncd,bhnsd->bhncs', q, k) * decay
    g_last = g[..., -1]
    q_dec = q * jnp.exp(g)[..., None]
    k_dec = k * jnp.exp(g_last[..., None] - g)[..., None]

    def step(s, xs):
        qd, kd, uc, wc, qkc, gl = xs
        v_new = uc - jnp.einsum('bhck,bhkv->bhcv', wc, s)
        o = jnp.einsum('bhck,bhkv->bhcv', qd, s) + jnp.einsum('bhcs,bhsv->bhcv', qkc, v_new)
        s = s * jnp.exp(gl)[..., None, None] + jnp.einsum('bhck,bhcv->bhkv', kd, v_new)
        return s, o

    xs = tuple(jnp.moveaxis(z, 2, 0) for z in (q_dec, k_dec, u, w, qk, g_last))
    s_fin, o = lax.scan(step, s0, xs)
    return jnp.moveaxis(o, 0, 2).reshape(b, h, l, -1), s_fin


def _mlstm_chunked(q, k, v, i_pre, logf, c0, n0, m0):
    b, h, l, dk = q.shape
    n = l // CHUNK

    def chunks(t):
        return t.reshape(b, h, n, CHUNK, *t.shape[3:])

    q = chunks(q * dk ** -0.5)
    k = chunks(k)
    v = chunks(v)
    i_pre = chunks(i_pre)
    bcum = jnp.cumsum(chunks(logf), axis=-1)
    incl = jnp.tril(jnp.ones((CHUNK, CHUNK), dtype=bool))
    logd = jnp.where(incl, bcum[..., :, None] - bcum[..., None, :] + i_pre[..., None, :], -jnp.inf)
    qk = jnp.einsum('bhncd,bhnsd->bhncs', q, k)
    b_last = bcum[..., -1]
    logw = b_last[..., None] - bcum + i_pre

    def step(carry, xs):
        cm, nm, m = carry
        qc, kc, vc, bc, ld, qkc, bl, lw = xs
        m_inter = bc + m[..., None]
        m_t = jnp.maximum(m_inter, jnp.max(ld, axis=-1))
        w_inter = jnp.exp(m_inter - m_t)
        s = qkc * jnp.exp(ld - m_t[..., None])
        num = (w_inter[..., None] * jnp.einsum('bhck,bhkv->bhcv', qc, cm)
               + jnp.einsum('bhcs,bhsv->bhcv', s, vc))
        den = w_inter * jnp.einsum('bhck,bhk->bhc', qc, nm) + jnp.sum(s, axis=-1)
        hc = num / jnp.maximum(jnp.abs(den), jnp.exp(-m_t))[..., None]
        m_new = jnp.maximum(bl + m, jnp.max(lw, axis=-1))
        dec = jnp.exp(bl + m - m_new)
        wk = kc * jnp.exp(lw - m_new[..., None])[..., None]
        cm = dec[..., None, None] * cm + jnp.einsum('bhck,bhcv->bhkv', wk, vc)
        nm = dec[..., None] * nm + jnp.sum(wk, axis=2)
        return (cm, nm, m_new), hc

    xs = tuple(jnp.moveaxis(z, 2, 0) for z in (q, k, v, bcum, logd, qk, b_last, logw))
    (cf, nf, mf), hs = lax.scan(step, (c0, n0, m0), xs)
    return jnp.moveaxis(hs, 0, 2).reshape(b, h, l, -1), cf, nf, mf


def _gdn_branch(qa, ka, va, za, ga, ba, p, s0):
    f32 = jnp.float32
    b, l, _ = qa.shape
    qkv = jax.nn.silu(_centred_dwconv(jnp.concatenate([qa, ka, va], axis=-1), p['gdn_conv']).astype(f32))
    q, k, v = jnp.split(qkv, [H_A * DK_A, 2 * H_A * DK_A], axis=-1)
    q = _l2norm(_heads(q, H_A))
    k = _l2norm(_heads(k, H_A))
    v = _heads(v, H_A)
    ga = ga.astype(f32).reshape(b, l, 2, H_A).transpose(2, 0, 3, 1)
    ba = ba.astype(f32).reshape(b, l, 2, H_A).transpose(2, 0, 3, 1)
    g = -jnp.exp(p['gdn_a_log'].astype(f32))[:, None, :, None] * jax.nn.softplus(
        ga + p['gdn_dt_bias'].astype(f32)[:, None, :, None])
    beta = jax.nn.sigmoid(ba)
    s0 = s0.astype(f32)
    o_f, s_f = _gdn_chunked(q, k, v, g[0], beta[0], s0[:, 0])
    o_b, s_b = _gdn_chunked(_flip(q), _flip(k), _flip(v), _flip(g[1]), _flip(beta[1]), s0[:, 1])
    o = _rmsnorm(o_f + _flip(o_b), p['gdn_norm'])
    out = _unheads(o) * jax.nn.silu(za.astype(f32))
    return out.astype(za.dtype), jnp.stack([s_f, s_b], axis=1)


def _mlstm_branch(qc, kc, vc, oc, zc, ic, fc, p, c0, n0, m0):
    f32 = jnp.float32
    b, l, _ = qc.shape
    q = _heads(qc.astype(f32), H_C)
    k = _heads(kc.astype(f32), H_C)
    v = _heads(vc.astype(f32), H_C)
    i_pre = (ic.astype(f32).reshape(b, l, 2, H_C).transpose(2, 0, 3, 1)
             + p['mlstm_b_i'].astype(f32)[:, None, :, None])
    logf = jax.nn.log_sigmoid(fc.astype(f32).reshape(b, l, 2, H_C).transpose(2, 0, 3, 1)
                              + p['mlstm_b_f'].astype(f32)[:, None, :, None])
    c0 = c0.astype(f32)
    n0 = n0.astype(f32)
    m0 = m0.astype(f32)
    h_f, c_f, n_f, m_f = _mlstm_chunked(q, k, v, i_pre[0], logf[0], c0[:, 0], n0[:, 0], m0[:, 0])
    h_b, c_b, n_b, m_b = _mlstm_chunked(_flip(q), _flip(k), _flip(v), _flip(i_pre[1]), _flip(logf[1]),
                                        c0[:, 1], n0[:, 1], m0[:, 1])
    hh = _rmsnorm(h_f + _flip(h_b), p['mlstm_norm'])
    out = _unheads(hh) * jax.nn.sigmoid(oc.astype(f32)) * jax.nn.silu(zc.astype(f32))
    states = (jnp.stack([c_f, c_b], axis=1), jnp.stack([n_f, n_b], axis=1), jnp.stack([m_f, m_b], axis=1))
    return out.astype(zc.dtype), states


def _mla_queries(cq, p, ang_row=None, ang_col=None):
    b, l, _ = cq.shape
    q = (_rmsnorm(cq, p['mla_q_norm']) @ p['mla_w_uq']).reshape(b, l, H_B, NOPE_DIM + ROPE_DIM)
    q_pe = q[..., NOPE_DIM:]
    if ang_row is not None:
        q_pe = _axial_rope(q_pe, ang_row, ang_col)
    return jnp.concatenate([q[..., :NOPE_DIM], q_pe], axis=-1) * SM_SCALE


def _mla_keys_values(ckv_n, k_rope, p):
    b, l, _ = ckv_n.shape
    kv = (ckv_n @ p['mla_w_ukv']).reshape(b, l, H_B, NOPE_DIM + DV_B)
    k_pe = jnp.broadcast_to(k_rope[:, :, None, :], (b, l, H_B, ROPE_DIM))
    return jnp.concatenate([kv[..., :NOPE_DIM], k_pe], axis=-1), kv[..., NOPE_DIM:]


def _block_attention(q, k, v):
    b, lq, h, dqk = q.shape
    nb = lq // Q_BLOCK
    qb = jnp.moveaxis(q.reshape(b, nb, Q_BLOCK, h, dqk), 1, 0)

    def one_block(qi):
        s = jnp.einsum('bqhd,bkhd->bhqk', qi, k).astype(jnp.float32)
        pr = jax.nn.softmax(s, axis=-1).astype(v.dtype)
        return jnp.einsum('bhqk,bkhd->bqhd', pr, v)

    o = lax.map(one_block, qb)
    return jnp.moveaxis(o, 0, 1).reshape(b, lq, h, v.shape[-1])


def _context_layer(x, c_ctx, p):
    b, l, _ = x.shape
    f32 = jnp.float32
    h, gate = _modulate(x, c_ctx[None, :], p)
    (qa, ka, va, za, ga, ba, cq, ckv, kr, zb,
     qc, kc, vc, oc, zc, ic, fc) = _split_proj(h, p['w_in'])
    out_a, s_gdn = _gdn_branch(qa, ka, va, za, ga, ba, p, jnp.zeros((b, 2, H_A, DK_A, DV_A), f32))
    ckv_n = _rmsnorm(ckv, p['mla_kv_norm'])
    k, v = _mla_keys_values(ckv_n, kr, p)
    att = _block_attention(_mla_queries(cq, p), k, v)
    out_b = att.reshape(b, l, W_B) * jax.nn.silu(zb)
    out_c, (cm, nm, mm) = _mlstm_branch(qc, kc, vc, oc, zc, ic, fc, p,
                                        jnp.zeros((b, 2, H_C, DK_C, DV_C), f32),
                                        jnp.zeros((b, 2, H_C, DK_C), f32),
                                        jnp.zeros((b, 2, H_C), f32))
    y = jnp.concatenate([out_a, out_b, out_c], axis=-1) @ p['w_out']
    x = x + gate * _rmsnorm(y, p['g_post'])
    return x, (ckv_n, kr, s_gdn, cm, nm, mm)


def _latent_layer(x, c, cache, p, ang_row, ang_col):
    ckv_ctx, kr_ctx, s_gdn0, c0, n0, m0 = cache
    b, l, _ = x.shape
    h, gate = _modulate(x, c, p)
    (qa, ka, va, za, ga, ba, cq, ckv, kr, zb,
     qc, kc, vc, oc, zc, ic, fc) = _split_proj(h, p['w_in'])
    out_a, _ = _gdn_branch(qa, ka, va, za, ga, ba, p, s_gdn0)
    q = _mla_queries(cq, p, ang_row[:, None, :], ang_col[:, None, :])
    k_ctx, v_ctx = _mla_keys_values(ckv_ctx.astype(x.dtype), kr_ctx.astype(x.dtype), p)
    k_lat, v_lat = _mla_keys_values(_rmsnorm(ckv, p['mla_kv_norm']), _axial_rope(kr, ang_row, ang_col), p)
    att = _block_attention(q, jnp.concatenate([k_ctx, k_lat], axis=1), jnp.concatenate([v_ctx, v_lat], axis=1))
    out_b = att.reshape(b, l, W_B) * jax.nn.silu(zb)
    out_c, _ = _mlstm_branch(qc, kc, vc, oc, zc, ic, fc, p, c0, n0, m0)
    y = jnp.concatenate([out_a, out_b, out_c], axis=-1) @ p['w_out']
    return x + gate * _rmsnorm(y, p['g_post'])


def setup_inputs(seed: int = 0) -> dict:
    key = jax.random.key(seed)
    ks = iter(jax.random.split(key, 32))
    f32 = jnp.float32

    def nrm(shape, scale=1.0):
        return scale * jax.random.normal(next(ks), shape, f32)

    def gain(shape):
        return 1.0 + 0.01 * jax.random.normal(next(ks), shape, f32)

    x_prompt = nrm((BATCH, SEQ, D_MODEL))
    x_sample = nrm((DEC_BATCH, DEC_SEQ, D_MODEL))
    cache_mla_ckv = nrm((DEC_BATCH, DEPTH, PAST_LEN, KV_LORA))
    cache_mla_krope = nrm((DEC_BATCH, DEPTH, PAST_LEN, ROPE_DIM))
    state_gdn = nrm((DEC_BATCH, DEPTH, 2, H_A, DK_A, DV_A), 0.1)
    state_mlstm_c = nrm((DEC_BATCH, DEPTH, 2, H_C, DK_C, DV_C), 0.1)
    state_mlstm_n = nrm((DEC_BATCH, DEPTH, 2, H_C, DK_C), 0.1)
    state_mlstm_m = nrm((DEC_BATCH, DEPTH, 2, H_C))
    c = nrm((DEC_BATCH, D_MODEL))
    c_ctx = nrm((D_MODEL,))
    w_ada = nrm((DEPTH, D_MODEL, 3 * D_MODEL), 0.5 * D_MODEL ** -0.5)
    b_ada = nrm((DEPTH, 3 * D_MODEL), 0.01)
    g_pre = gain((DEPTH, D_MODEL))
    g_post = gain((DEPTH, D_MODEL))
    w_in = nrm((DEPTH, D_MODEL, N_IN), D_MODEL ** -0.5)
    gdn_conv = nrm((DEPTH, CONV_K, 2 * H_A * DK_A + W_A), CONV_K ** -0.5)
    gdn_a_log = jnp.log(jax.random.uniform(next(ks), (DEPTH, 2, H_A), f32, 1.0, 16.0))
    dt = jnp.exp(jax.random.uniform(next(ks), (DEPTH, 2, H_A), f32, math.log(1e-3), math.log(1e-1)))
    gdn_dt_bias = dt + jnp.log(-jnp.expm1(-dt))
    gdn_norm = gain((DEPTH, DV_A))
    mla_q_norm = gain((DEPTH, Q_LORA))
    mla_kv_norm = gain((DEPTH, KV_LORA))
    mla_w_uq = nrm((DEPTH, Q_LORA, H_B * (NOPE_DIM + ROPE_DIM)), Q_LORA ** -0.5)
    mla_w_ukv = nrm((DEPTH, KV_LORA, H_B * (NOPE_DIM + DV_B)), KV_LORA ** -0.5)
    mlstm_b_i = nrm((DEPTH, 2, H_C), 0.1)
    mlstm_b_f = 3.0 + 3.0 * jax.random.uniform(next(ks), (DEPTH, 2, H_C), f32)
    mlstm_norm = gain((DEPTH, DV_C))
    w_out = nrm((DEPTH, MIX_W, D_MODEL), MIX_W ** -0.5)
    return {'x_prompt': x_prompt, 'x_sample': x_sample,
            'cache_mla_ckv': cache_mla_ckv, 'cache_mla_krope': cache_mla_krope,
            'state_gdn': state_gdn, 'state_mlstm_c': state_mlstm_c,
            'state_mlstm_n': state_mlstm_n, 'state_mlstm_m': state_mlstm_m,
            'c': c, 'c_ctx': c_ctx, 'w_ada': w_ada, 'b_ada': b_ada,
            'g_pre': g_pre, 'g_post': g_post, 'w_in': w_in,
            'gdn_conv': gdn_conv, 'gdn_a_log': gdn_a_log, 'gdn_dt_bias': gdn_dt_bias,
            'gdn_norm': gdn_norm, 'mla_q_norm': mla_q_norm, 'mla_kv_norm': mla_kv_norm,
            'mla_w_uq': mla_w_uq, 'mla_w_ukv': mla_w_ukv,
            'mlstm_b_i': mlstm_b_i, 'mlstm_b_f': mlstm_b_f, 'mlstm_norm': mlstm_norm,
            'w_out': w_out}


def reference(x_prompt, x_sample, cache_mla_ckv, cache_mla_krope, state_gdn, state_mlstm_c,
              state_mlstm_n, state_mlstm_m, c, c_ctx, w_ada, b_ada, g_pre, g_post, w_in,
              gdn_conv, gdn_a_log, gdn_dt_bias, gdn_norm, mla_q_norm, mla_kv_norm,
              mla_w_uq, mla_w_ukv, mlstm_b_i, mlstm_b_f, mlstm_norm, w_out):
    rows = x_sample.shape[1] // GRID_W
    ang_row, ang_col = _axial_angles(rows)
    xp = x_prompt
    xs = x_sample
    ctx_states = []
    for l in range(DEPTH):
        p = {'w_ada': w_ada[l], 'b_ada': b_ada[l], 'g_pre': g_pre[l], 'g_post': g_post[l],
             'w_in': w_in[l], 'gdn_conv': gdn_conv[l], 'gdn_a_log': gdn_a_log[l],
             'gdn_dt_bias': gdn_dt_bias[l], 'gdn_norm': gdn_norm[l],
             'mla_q_norm': mla_q_norm[l], 'mla_kv_norm': mla_kv_norm[l],
             'mla_w_uq': mla_w_uq[l], 'mla_w_ukv': mla_w_ukv[l],
             'mlstm_b_i': mlstm_b_i[l], 'mlstm_b_f': mlstm_b_f[l], 'mlstm_norm': mlstm_norm[l],
             'w_out': w_out[l]}
        xp, st = _context_layer(xp, c_ctx, p)
        ctx_states.append(st)
        cache_l = (cache_mla_ckv[:, l], cache_mla_krope[:, l], state_gdn[:, l],
                   state_mlstm_c[:, l], state_mlstm_n[:, l], state_mlstm_m[:, l])
        xs = _latent_layer(xs, c, cache_l, p, ang_row, ang_col)
    new_mla_ckv = jnp.stack([s[0] for s in ctx_states], axis=1)
    new_mla_krope = jnp.stack([s[1] for s in ctx_states], axis=1)
    new_gdn = jnp.stack([s[2] for s in ctx_states], axis=1)
    new_mlstm_c = jnp.stack([s[3] for s in ctx_states], axis=1)
    new_mlstm_n = jnp.stack([s[4] for s in ctx_states], axis=1)
    new_mlstm_m = jnp.stack([s[5] for s in ctx_states], axis=1)
    return (xp, xs, new_mla_ckv, new_mla_krope, new_gdn, new_mlstm_c, new_mlstm_n, new_mlstm_m)
```

```python
import functools
import math

import jax
import jax.numpy as jnp
import numpy as np
from jax import lax
from jax.experimental import pallas as pl
from jax.experimental.pallas import tpu as pltpu

F32 = jnp.float32
BF16 = jnp.bfloat16
HIGHEST = lax.Precision.HIGHEST

D_MODEL = 2048
DEPTH = 2
GRID_W = 64
EPS = 1e-6
CHUNK = 64
H_A = 4
H_B = 8
H_C = 4
HEAD = 128
Q_LORA = 512
KV_LORA = 256
ROPE_DIM = 64
AXIS_ROPE = ROPE_DIM // 2
ROPE_THETA = 10000.0
SM_SCALE = (HEAD + ROPE_DIM) ** -0.5
QK_SCALE = HEAD ** -0.5
W_A = H_A * HEAD
W_B = H_B * HEAD
W_C = H_C * HEAD
LANES = 128
QK_PAD = 256

COL_GDN = 0
COL_ZB = 2048
COL_CQ = 3072
COL_MLSTM = 3584
COL_CKV = 6144
COL_ROPE = 6400
COL_GATES = 6528
N_PROJ = 6656
VMEM_LIMIT = 56 * 1024 * 1024


def _cparams(sem):
    return pltpu.CompilerParams(dimension_semantics=sem, vmem_limit_bytes=VMEM_LIMIT)


def _tile(n, pref):
    t = min(n, pref)
    while n % t:
        t //= 2
    return t


def _bdot(a, b):
    return jnp.dot(a.astype(BF16), b.astype(BF16), preferred_element_type=F32)


def _bdot_nt(a, b):
    return lax.dot_general(a.astype(BF16), b.astype(BF16), (((1,), (1,)), ((), ())),
                           preferred_element_type=F32)


def _bdot_tn(a, b):
    return lax.dot_general(a.astype(BF16), b.astype(BF16), (((0,), (0,)), ((), ())),
                           preferred_element_type=F32)


def _fdot(a, b):
    return jnp.dot(a, b, precision=HIGHEST, preferred_element_type=F32)


def _split3(x):
    x1 = x.astype(BF16)
    r1 = x - x1.astype(F32)
    x2 = r1.astype(BF16)
    x3 = (r1 - x2.astype(F32)).astype(BF16)
    return x1, x2, x3


def _dot01_right(x, m01):
    m = m01.astype(BF16)
    x1, x2, x3 = _split3(x)
    return (jnp.dot(x1, m, preferred_element_type=F32) + jnp.dot(x2, m, preferred_element_type=F32)
            + jnp.dot(x3, m, preferred_element_type=F32))


def _dot01_left(m01, x):
    m = m01.astype(BF16)
    x1, x2, x3 = _split3(x)
    return (jnp.dot(m, x1, preferred_element_type=F32) + jnp.dot(m, x2, preferred_element_type=F32)
            + jnp.dot(m, x3, preferred_element_type=F32))


def _rms(x, g):
    return x * lax.rsqrt(jnp.mean(x * x, axis=-1, keepdims=True) + EPS) * g


def _silu(x):
    return x * jax.nn.sigmoid(x)


def _softplus(x):
    return jnp.maximum(x, 0.0) + jnp.log1p(jnp.exp(-jnp.abs(x)))


def _tri_inv(a, r, c):
    eye = (r == c).astype(F32)
    a0 = jnp.where((r >> 3) == (c >> 3), a, 0.0)
    x = eye - a0
    a2 = _fdot(a0, a0)
    x = x + _fdot(x, a2)
    a4 = _fdot(a2, a2)
    x = x + _fdot(x, a4)
    for sh in (4, 5, 6):
        off = jnp.where(((r >> sh) == (c >> sh)) != ((r >> (sh - 1)) == (c >> (sh - 1))), a, 0.0)
        x = x - _fdot(_fdot(x, off), x)
    return x


def _dir_masks(r, c, reverse):
    if reverse:
        return r <= c, r < c
    return r >= c, r > c


def _mod_kernel(c_ref, w_ref, b_ref, o_ref):
    o_ref[0] = _bdot(_silu(c_ref[...]), w_ref[0]) + b_ref[0]


def _mod_call(cond, w_ada, b_ada):
    n = w_ada.shape[-1]
    tn = _tile(n, 1536)
    return pl.pallas_call(
        _mod_kernel,
        out_shape=jax.ShapeDtypeStruct((DEPTH, cond.shape[0], n), F32),
        grid=(DEPTH, n // tn),
        in_specs=[pl.BlockSpec(cond.shape, lambda l, j: (0, 0)),
                  pl.BlockSpec((1, D_MODEL, tn), lambda l, j: (l, 0, j)),
                  pl.BlockSpec((1, 1, tn), lambda l, j: (l, 0, j))],
        out_specs=pl.BlockSpec((1, cond.shape[0], tn), lambda l, j: (l, 0, j)),
        compiler_params=_cparams(("parallel", "parallel")),
        name="adaln_mod",
    )(cond, w_ada, b_ada.reshape(DEPTH, 1, n))


def _inproj_kernel(x_ref, g_ref, sh_ref, sc_ref, w_ref, o_ref, h_scr):
    @pl.when(pl.program_id(2) == 0)
    def _():
        h = _rms(x_ref[0], g_ref[...]) * (1.0 + sc_ref[0]) + sh_ref[0]
        h_scr[...] = h.astype(BF16)

    o_ref[0] = jnp.dot(h_scr[...], w_ref[...], preferred_element_type=F32)


def _inproj_call(x, g_pre, mod, row0, w):
    b, l, _ = x.shape
    tm = _tile(l, 512)
    tn = N_PROJ // 4
    return pl.pallas_call(
        _inproj_kernel,
        out_shape=jax.ShapeDtypeStruct((b, l, N_PROJ), F32),
        grid=(b, l // tm, N_PROJ // tn),
        in_specs=[pl.BlockSpec((1, tm, D_MODEL), lambda i, j, k: (i, j, 0)),
                  pl.BlockSpec((1, D_MODEL), lambda i, j, k: (0, 0)),
                  pl.BlockSpec((1, 1, D_MODEL), lambda i, j, k: (i + row0, 0, 0)),
                  pl.BlockSpec((1, 1, D_MODEL), lambda i, j, k: (i + row0, 0, 1)),
                  pl.BlockSpec((D_MODEL, tn), lambda i, j, k: (0, k))],
        out_specs=pl.BlockSpec((1, tm, tn), lambda i, j, k: (i, j, k)),
        scratch_shapes=[pltpu.VMEM((tm, D_MODEL), BF16)],
        compiler_params=_cparams(("parallel", "parallel", "arbitrary")),
        name="in_proj",
    )(x, g_pre, mod, mod, w)


def _outproj_kernel(a_ref, b_ref, c_ref, w_ref, x_ref, gate_ref, g_ref, o_ref):
    y = (jnp.dot(a_ref[0], w_ref[0:W_A], preferred_element_type=F32)
         + jnp.dot(b_ref[0], w_ref[W_A:W_A + W_B], preferred_element_type=F32)
         + jnp.dot(c_ref[0], w_ref[W_A + W_B:], preferred_element_type=F32))
    o_ref[0] = x_ref[0] + gate_ref[0] * _rms(y, g_ref[...])


def _outproj_call(mix_a, mix_b, mix_c, w, x, mod, row0, g_post):
    b, l, _ = x.shape
    tm = _tile(l, 512)
    return pl.pallas_call(
        _outproj_kernel,
        out_shape=jax.ShapeDtypeStruct(x.shape, F32),
        grid=(b, l // tm),
        in_specs=[pl.BlockSpec((1, tm, W_A), lambda i, j: (i, j, 0)),
                  pl.BlockSpec((1, tm, W_B), lambda i, j: (i, j, 0)),
                  pl.BlockSpec((1, tm, W_C), lambda i, j: (i, j, 0)),
                  pl.BlockSpec((D_MODEL, D_MODEL), lambda i, j: (0, 0)),
                  pl.BlockSpec((1, tm, D_MODEL), lambda i, j: (i, j, 0)),
                  pl.BlockSpec((1, 1, D_MODEL), lambda i, j: (i + row0, 0, 2)),
                  pl.BlockSpec((1, D_MODEL), lambda i, j: (0, 0))],
        out_specs=pl.BlockSpec((1, tm, D_MODEL), lambda i, j: (i, j, 0)),
        compiler_params=_cparams(("parallel", "parallel")),
        name="out_proj",
    )(mix_a, mix_b, mix_c, w, x, mod, g_post)


def _rope_pair(x, cs):
    y = x * cs
    return y + pltpu.roll(y, ROPE_DIM, axis=1)


def _store_kv(kv, kpe, k_out, v_out):
    kpe = kpe.astype(BF16)
    for h in range(H_B):
        k_out[0, :, h * QK_PAD:h * QK_PAD + HEAD] = kv[:, 2 * h * HEAD:(2 * h + 1) * HEAD].astype(BF16)
        k_out[0, :, h * QK_PAD + HEAD:(h + 1) * QK_PAD] = kpe
        v_out[0, :, h * HEAD:(h + 1) * HEAD] = kv[:, (2 * h + 1) * HEAD:(2 * h + 2) * HEAD].astype(BF16)


def _mla_prep_kernel(cq_ref, ckv_ref, kr_ref, cs_ref, qn_ref, kvn_ref, wq_ref, wkv_ref,
                     q_out, k_out, v_out, *cache_out):
    cs = cs_ref[...]
    q = _bdot(_rms(cq_ref[0], qn_ref[...]), wq_ref[...])
    for h in range(H_B):
        q_out[0, :, h * QK_PAD:h * QK_PAD + HEAD] = (
            q[:, h * QK_PAD:h * QK_PAD + HEAD] * SM_SCALE).astype(BF16)
        q_out[0, :, h * QK_PAD + HEAD:(h + 1) * QK_PAD] = (
            _rope_pair(q[:, h * QK_PAD + HEAD:(h + 1) * QK_PAD], cs) * SM_SCALE).astype(BF16)
    ckvn = _rms(ckv_ref[0], kvn_ref[...])
    kr = kr_ref[0]
    lane = lax.broadcasted_iota(jnp.int32, kr.shape, 1)
    kpe = jnp.where(lane < ROPE_DIM, _rope_pair(kr, cs), 0.0)
    _store_kv(_bdot(ckvn, wkv_ref[...]), kpe, k_out, v_out)
    if cache_out:
        cache_out[0][0] = ckvn
        cache_out[1][0] = kr[:, :ROPE_DIM]


def _mla_prep_call(proj, cs, q_norm, kv_norm, wq, wkv, emit_cache):
    b, l, _ = proj.shape
    tm = _tile(l, 512)
    out_shape = [jax.ShapeDtypeStruct((b, l, H_B * QK_PAD), BF16),
                 jax.ShapeDtypeStruct((b, l, H_B * QK_PAD), BF16),
                 jax.ShapeDtypeStruct((b, l, H_B * HEAD), BF16)]
    out_specs = [pl.BlockSpec((1, tm, H_B * QK_PAD), lambda i, j: (i, j, 0)),
                 pl.BlockSpec((1, tm, H_B * QK_PAD), lambda i, j: (i, j, 0)),
                 pl.BlockSpec((1, tm, H_B * HEAD), lambda i, j: (i, j, 0))]
    if emit_cache:
        out_shape += [jax.ShapeDtypeStruct((b, l, KV_LORA), F32),
                      jax.ShapeDtypeStruct((b, l, ROPE_DIM), F32)]
        out_specs += [pl.BlockSpec((1, tm, KV_LORA), lambda i, j: (i, j, 0)),
                      pl.BlockSpec((1, tm, ROPE_DIM), lambda i, j: (i, j, 0))]
    return pl.pallas_call(
        _mla_prep_kernel,
        out_shape=out_shape,
        grid=(b, l // tm),
        in_specs=[pl.BlockSpec((1, tm, Q_LORA), lambda i, j: (i, j, COL_CQ // Q_LORA)),
                  pl.BlockSpec((1, tm, KV_LORA), lambda i, j: (i, j, COL_CKV // KV_LORA)),
                  pl.BlockSpec((1, tm, LANES), lambda i, j: (i, j, COL_ROPE // LANES)),
                  pl.BlockSpec((tm, LANES), lambda i, j: (j, 0)),
                  pl.BlockSpec((1, Q_LORA), lambda i, j: (0, 0)),
                  pl.BlockSpec((1, KV_LORA), lambda i, j: (0, 0)),
                  pl.BlockSpec((Q_LORA, H_B * QK_PAD), lambda i, j: (0, 0)),
                  pl.BlockSpec((KV_LORA, 2 * H_B * HEAD), lambda i, j: (0, 0))],
        out_specs=out_specs,
        compiler_params=_cparams(("parallel", "parallel")),
        name="mla_prep",
    )(proj, proj, proj, cs, q_norm, kv_norm, wq, wkv)


def _kv_cache_kernel(ckv_ref, kr_ref, wkv_ref, k_out, v_out):
    _store_kv(_bdot(ckv_ref[0], wkv_ref[...]), kr_ref[0], k_out, v_out)


def _kv_cache_call(ckv, kr_pad, wkv):
    b, p, _ = ckv.shape
    return pl.pallas_call(
        _kv_cache_kernel,
        out_shape=[jax.ShapeDtypeStruct((b, p, H_B * QK_PAD), BF16),
                   jax.ShapeDtypeStruct((b, p, H_B * HEAD), BF16)],
        grid=(b,),
        in_specs=[pl.BlockSpec((1, p, KV_LORA), lambda i: (i, 0, 0)),
                  pl.BlockSpec((1, p, LANES), lambda i: (i, 0, 0)),
                  pl.BlockSpec((KV_LORA, 2 * H_B * HEAD), lambda i: (0, 0))],
        out_specs=[pl.BlockSpec((1, p, H_B * QK_PAD), lambda i: (i, 0, 0)),
                   pl.BlockSpec((1, p, H_B * HEAD), lambda i: (i, 0, 0))],
        compiler_params=_cparams(("parallel",)),
        name="mla_cache_kv",
    )(ckv, kr_pad, wkv)


def _attn_kernel(*refs, n_parts):
    q_ref = refs[0]
    kv_refs = refs[1:1 + 2 * n_parts]
    z_ref = refs[1 + 2 * n_parts]
    o_ref = refs[2 + 2 * n_parts]
    q = q_ref[0]
    scores = [lax.dot_general(q, kv_refs[2 * p][0], (((1,), (1,)), ((), ())),
                              preferred_element_type=F32) for p in range(n_parts)]
    m = functools.reduce(jnp.maximum, [jnp.max(s, axis=-1, keepdims=True) for s in scores])
    acc = None
    den = None
    for p in range(n_parts):
        e = jnp.exp(scores[p] - m)
        d = jnp.sum(e, axis=-1, keepdims=True)
        a = jnp.dot(e.astype(BF16), kv_refs[2 * p + 1][0], preferred_element_type=F32)
        acc = a if acc is None else acc + a
        den = d if den is None else den + d
    o_ref[0] = (acc / den * _silu(z_ref[0])).astype(BF16)


def _attn_call(q, kv_parts, proj):
    b, l, _ = q.shape
    tq = _tile(l, 256)
    in_specs = [pl.BlockSpec((1, tq, QK_PAD), lambda i, h, j: (i, j, h))]
    args = [q]
    for k, v in kv_parts:
        lk = k.shape[1]
        in_specs += [pl.BlockSpec((1, lk, QK_PAD), lambda i, h, j: (i, 0, h)),
                     pl.BlockSpec((1, lk, HEAD), lambda i, h, j: (i, 0, h))]
        args += [k, v]
    in_specs.append(pl.BlockSpec((1, tq, HEAD), lambda i, h, j: (i, j, COL_ZB // HEAD + h)))
    args.append(proj)
    return pl.pallas_call(
        functools.partial(_attn_kernel, n_parts=len(kv_parts)),
        out_shape=jax.ShapeDtypeStruct((b, l, W_B), BF16),
        grid=(b, H_B, l // tq),
        in_specs=in_specs,
        out_specs=pl.BlockSpec((1, tq, HEAD), lambda i, h, j: (i, j, h)),
        compiler_params=_cparams(("parallel", "parallel", "arbitrary")),
        name="mla_attention",
    )(*args)


def _chunk_consts(head, lane0_a, lane0_b):
    r = lax.broadcasted_iota(jnp.int32, (CHUNK, CHUNK), 0)
    c = lax.broadcasted_iota(jnp.int32, (CHUNK, CHUNK), 1)
    sr = lax.broadcasted_iota(jnp.int32, (LANES, 2 * LANES), 0)
    sc = lax.broadcasted_iota(jnp.int32, (LANES, 2 * LANES), 1)
    sel = []
    for d in range(2):
        want = jnp.where(sc < LANES, lane0_a + 4 * d + head, lane0_b + 4 * d + head)
        sel.append((sr == want).astype(F32))
    return r, c, sel


def _cumsum_pair(x, r, c, reverse):
    incl, _ = _dir_masks(r, c, reverse)
    other, _ = _dir_masks(r, c, not reverse)
    col = _dot01_left(incl.astype(F32), x)
    row = _dot01_left(jnp.ones((CHUNK, CHUNK), F32), jnp.where(other, x[:, :CHUNK], 0.0))
    return col, row


def _gdn_kernel(*refs, seq, has_init, emit_state):
    q_ref, k_ref, v_ref, z_ref, gt_ref, cwq_ref, cwk_ref, cwv_ref, par_ref, norm_ref = refs[:10]
    pos = 10
    s0_ref = None
    if has_init:
        s0_ref = refs[pos]
        pos += 1
    o_ref = refs[pos]
    pos += 1
    sfin_ref = None
    if emit_state:
        sfin_ref = refs[pos]
        pos += 1
    qn_s, kn_s, vn_s, of_s, ob_s = refs[pos:pos + 5]

    n = seq // CHUNK
    head = pl.program_id(1)
    r, c, sel = _chunk_consts(head, 0, 8)
    lane = lax.broadcasted_iota(jnp.int32, (CHUNK, LANES), 1)
    bias = par_ref[0:1, :]
    neg_a = -jnp.exp(par_ref[1:2, :])

    def conv_silu(ref, w_ref, j, start):
        cur = ref[0, pl.ds(start, CHUNK), :]
        prev = ref[0, pl.ds(jnp.maximum(start - 8, 0), 8), :]
        prev = jnp.where(j > 0, prev, 0.0)
        nxt = ref[0, pl.ds(jnp.minimum(start + CHUNK, seq - 8), 8), :]
        nxt = jnp.where(j < n - 1, nxt, 0.0)
        win = jnp.concatenate([prev, cur, nxt], axis=0)
        w = w_ref[...]
        y = w[0:1, :] * win[6:6 + CHUNK]
        for t in range(1, 5):
            y = y + w[t:t + 1, :] * win[6 + t:6 + t + CHUNK]
        return _silu(y)

    def l2n(x):
        return x * lax.rsqrt(jnp.sum(x * x, axis=-1, keepdims=True) + EPS)

    def pre(j, carry):
        start = pl.multiple_of(j * CHUNK, CHUNK)
        qn_s[pl.ds(start, CHUNK), :] = l2n(conv_silu(q_ref, cwq_ref, j, start)) * QK_SCALE
        kn_s[pl.ds(start, CHUNK), :] = l2n(conv_silu(k_ref, cwk_ref, j, start))
        vn_s[pl.ds(start, CHUNK), :] = conv_silu(v_ref, cwv_ref, j, start)
        return carry

    lax.fori_loop(0, n, pre, 0)

    def chunk_dir(ci, d, s):
        reverse = d == 1
        incl, strict = _dir_masks(r, c, reverse)
        start = pl.multiple_of(ci * CHUNK, CHUNK)
        q = qn_s[pl.ds(start, CHUNK), :]
        k = kn_s[pl.ds(start, CHUNK), :]
        v = vn_s[pl.ds(start, CHUNK), :]
        x = gt_ref[0, pl.ds(start, CHUNK), :] + bias
        gates = jnp.where(lane < 8, neg_a * _softplus(x), jax.nn.sigmoid(x))
        gb = _dot01_right(gates, sel[d])
        g = gb[:, :LANES]
        beta = gb[:, LANES:]
        gc, grow = _cumsum_pair(g, r, c, reverse)
        decay = jnp.exp(jnp.where(incl, gc[:, :CHUNK] - grow, -jnp.inf))
        glast = gc[0:1, :] if reverse else gc[CHUNK - 1:CHUNK, :]
        kb = k * beta
        a = jnp.where(strict, _bdot_nt(kb, k) * decay, 0.0)
        t = _tri_inv(a, r, c)
        eg = jnp.exp(gc)
        uw = _bdot(t, jnp.concatenate([v * beta, kb * eg], axis=1))
        qk = _bdot_nt(q, k) * decay
        ws = _bdot(jnp.concatenate([uw[:, LANES:], q * eg], axis=0), s)
        v_new = uw[:, :LANES] - ws[:CHUNK]
        o = ws[CHUNK:] + _bdot(qk, v_new)
        s = s * jnp.exp(glast) + _bdot_tn(k * jnp.exp(glast - gc), v_new)
        return o, s

    def body(i, carry):
        s_f, s_b = carry
        o_f, s_f = chunk_dir(i, 0, s_f)
        of_s[pl.ds(pl.multiple_of(i * CHUNK, CHUNK), CHUNK), :] = o_f
        ib = n - 1 - i
        o_b, s_b = chunk_dir(ib, 1, s_b)
        ob_s[pl.ds(pl.multiple_of(ib * CHUNK, CHUNK), CHUNK), :] = o_b
        return s_f, s_b

    if has_init:
        init = (s0_ref[0, 0, 0], s0_ref[0, 1, 0])
    else:
        init = (jnp.zeros((HEAD, HEAD), F32), jnp.zeros((HEAD, HEAD), F32))
    s_f, s_b = lax.fori_loop(0, n, body, init)
    if emit_state:
        sfin_ref[0, 0, 0] = s_f
        sfin_ref[0, 1, 0] = s_b

    def post(j, carry):
        sl = pl.ds(pl.multiple_of(j * CHUNK, CHUNK), CHUNK)
        o = _rms(of_s[sl, :] + ob_s[sl, :], norm_ref[...])
        o_ref[0, sl, :] = (o * _silu(z_ref[0, sl, :])).astype(BF16)
        return carry

    lax.fori_loop(0, n, post, 0)


def _gdn_call(proj, conv_w, par, norm, s0, emit_state):
    b, l, _ = proj.shape
    base = COL_GDN // HEAD

    def col(k):
        return pl.BlockSpec((1, l, HEAD), lambda i, h: (i, 0, base + k * H_A + h))

    in_specs = [col(0), col(1), col(2), col(3),
                pl.BlockSpec((1, l, LANES), lambda i, h: (i, 0, COL_GATES // LANES)),
                pl.BlockSpec((5, HEAD), lambda i, h: (0, h)),
                pl.BlockSpec((5, HEAD), lambda i, h: (0, H_A + h)),
                pl.BlockSpec((5, HEAD), lambda i, h: (0, 2 * H_A + h)),
                pl.BlockSpec((8, LANES), lambda i, h: (0, 0)),
                pl.BlockSpec((1, HEAD), lambda i, h: (0, 0))]
    args = [proj, proj, proj, proj, proj, conv_w, conv_w, conv_w, par, norm]
    state_spec = pl.BlockSpec((1, 2, 1, HEAD, HEAD), lambda i, h: (i, 0, h, 0, 0))
    if s0 is not None:
        in_specs.append(state_spec)
        args.append(s0)
    out_shape = [jax.ShapeDtypeStruct((b, l, W_A), BF16)]
    out_specs = [pl.BlockSpec((1, l, HEAD), lambda i, h: (i, 0, h))]
    if emit_state:
        out_shape.append(jax.ShapeDtypeStruct((b, 2, H_A, HEAD, HEAD), F32))
        out_specs.append(state_spec)
    return pl.pallas_call(
        functools.partial(_gdn_kernel, seq=l, has_init=s0 is not None, emit_state=emit_state),
        out_shape=out_shape,
        grid=(b, H_A),
        in_specs=in_specs,
        out_specs=out_specs,
        scratch_shapes=[pltpu.VMEM((l, HEAD), F32) for _ in range(5)],
        compiler_params=_cparams(("parallel", "parallel")),
        name="gdn",
    )(*args)


def _mlstm_kernel(*refs, seq, has_init, emit_state):
    q_ref, k_ref, v_ref, og_ref, z_ref, gt_ref, par_ref, norm_ref = refs[:8]
    pos = 8
    if has_init:
        c0_ref, n0_ref, m0_ref = refs[pos:pos + 3]
        pos += 3
    o_ref = refs[pos]
    pos += 1
    if emit_state:
        cfin_ref, nfin_ref, mfin_ref = refs[pos:pos + 3]
        pos += 3
    hf_s, hb_s = refs[pos:pos + 2]

    n = seq // CHUNK
    head = pl.program_id(1)
    r, c, sel = _chunk_consts(head, 16, 24)
    lane = lax.broadcasted_iota(jnp.int32, (CHUNK, LANES), 1)
    bias = par_ref[0:1, :]
    eye = r == c
    ones = jnp.ones((CHUNK, LANES), F32)

    def dup(x):
        return jnp.concatenate([x, x], axis=1)

    def chunk_dir(ci, d, cn, m):
        reverse = d == 1
        incl, _ = _dir_masks(r, c, reverse)
        other, _ = _dir_masks(r, c, not reverse)
        sl = pl.ds(pl.multiple_of(ci * CHUNK, CHUNK), CHUNK)
        q = q_ref[0, sl, :] * QK_SCALE
        k = k_ref[0, sl, :]
        v2 = jnp.concatenate([v_ref[0, sl, :], ones], axis=1)
        x = gt_ref[0, sl, :] + bias
        gates = jnp.where(lane < 24, x, -_softplus(-x))
        gb = _dot01_right(gates, sel[d])
        ip = gb[:, :LANES]
        lf = gb[:, LANES:]
        bc = _dot01_left(incl.astype(F32), lf)
        rowp = _dot01_left(jnp.ones((CHUNK, CHUNK), F32),
                           jnp.where(eye, ip[:, :CHUNK], 0.0) - jnp.where(other, lf[:, :CHUNK], 0.0))
        logd = jnp.where(incl, bc[:, :CHUNK] + rowp, -jnp.inf)
        blast = bc[0:1, :] if reverse else bc[CHUNK - 1:CHUNK, :]
        logw = blast - bc + ip
        m_inter = bc + m
        m_t = jnp.maximum(m_inter, jnp.max(logd, axis=-1, keepdims=True))
        w_inter = jnp.exp(m_inter - m_t)
        sw = _bdot_nt(q, k) * jnp.exp(logd - m_t[:, :CHUNK])
        num2 = dup(w_inter) * _bdot(q, cn) + _bdot(sw, v2)
        hc = num2[:, :LANES] / jnp.maximum(jnp.abs(num2[:, LANES:]), jnp.exp(-m_t))
        m_new = jnp.maximum(blast + m, jnp.max(logw, axis=0, keepdims=True))
        dec = jnp.exp(blast + m - m_new)
        wk = k * jnp.exp(logw - m_new)
        cn = dup(dec) * cn + _bdot_tn(wk, v2)
        return hc, cn, m_new

    def body(i, carry):
        cn_f, m_f, cn_b, m_b = carry
        h_f, cn_f, m_f = chunk_dir(i, 0, cn_f, m_f)
        hf_s[pl.ds(pl.multiple_of(i * CHUNK, CHUNK), CHUNK), :] = h_f
        ib = n - 1 - i
        h_b, cn_b, m_b = chunk_dir(ib, 1, cn_b, m_b)
        hb_s[pl.ds(pl.multiple_of(ib * CHUNK, CHUNK), CHUNK), :] = h_b
        return cn_f, m_f, cn_b, m_b

    if has_init:
        init = []
        for d in range(2):
            nb = jnp.broadcast_to(n0_ref[0, d, 0], (HEAD, LANES))
            init += [jnp.concatenate([c0_ref[0, d, 0], nb], axis=1), m0_ref[0, d, 0]]
        init = tuple(init)
    else:
        init = (jnp.zeros((HEAD, 2 * LANES), F32), jnp.zeros((1, LANES), F32)) * 2
    fin = lax.fori_loop(0, n, body, init)
    if emit_state:
        for d in range(2):
            cn, m = fin[2 * d], fin[2 * d + 1]
            cfin_ref[0, d, 0] = cn[:, :LANES]
            nfin_ref[0, d, 0] = jnp.transpose(cn[:, LANES:])[0:1, :]
            mfin_ref[0, d, 0] = m

    def post(j, carry):
        sl = pl.ds(pl.multiple_of(j * CHUNK, CHUNK), CHUNK)
        hh = _rms(hf_s[sl, :] + hb_s[sl, :], norm_ref[...])
        o_ref[0, sl, :] = (hh * jax.nn.sigmoid(og_ref[0, sl, :]) * _silu(z_ref[0, sl, :])).astype(BF16)
        return carry

    lax.fori_loop(0, n, post, 0)


def _mlstm_call(proj, par, norm, init, emit_state):
    b, l, _ = proj.shape
    base = COL_MLSTM // HEAD

    def col(k):
        return pl.BlockSpec((1, l, HEAD), lambda i, h: (i, 0, base + k * H_C + h))

    in_specs = [col(0), col(1), col(2), col(3), col(4),
                pl.BlockSpec((1, l, LANES), lambda i, h: (i, 0, COL_GATES // LANES)),
                pl.BlockSpec((8, LANES), lambda i, h: (0, 0)),
                pl.BlockSpec((1, HEAD), lambda i, h: (0, 0))]
    args = [proj, proj, proj, proj, proj, proj, par, norm]
    c_spec = pl.BlockSpec((1, 2, 1, HEAD, HEAD), lambda i, h: (i, 0, h, 0, 0))
    row_spec = pl.BlockSpec((1, 2, 1, 1, LANES), lambda i, h: (i, 0, h, 0, 0))
    if init is not None:
        in_specs += [c_spec, pl.BlockSpec((1, 2, 1, HEAD, 1), lambda i, h: (i, 0, h, 0, 0)), row_spec]
        args += list(init)
    out_shape = [jax.ShapeDtypeStruct((b, l, W_C), BF16)]
    out_specs = [pl.BlockSpec((1, l, HEAD), lambda i, h: (i, 0, h))]
    if emit_state:
        out_shape += [jax.ShapeDtypeStruct((b, 2, H_C, HEAD, HEAD), F32),
                      jax.ShapeDtypeStruct((b, 2, H_C, 1, LANES), F32),
                      jax.ShapeDtypeStruct((b, 2, H_C, 1, LANES), F32)]
        out_specs += [c_spec, row_spec, row_spec]
    return pl.pallas_call(
        functools.partial(_mlstm_kernel, seq=l, has_init=init is not None, emit_state=emit_state),
        out_shape=out_shape,
        grid=(b, H_C),
        in_specs=in_specs,
        out_specs=out_specs,
        scratch_shapes=[pltpu.VMEM((l, HEAD), F32) for _ in range(2)],
        compiler_params=_cparams(("parallel", "parallel")),
        name="mlstm",
    )(*args)


def _rope_swap(w):
    q = AXIS_ROPE // 2
    return jnp.concatenate([-w[..., q:2 * q], w[..., 0:q], -w[..., 3 * q:4 * q], w[..., 2 * q:3 * q]], axis=-1)


def _prep_w_in(w):
    qa_za, ga, ba = w[:, 0:2048], w[:, 2048:2056], w[:, 2056:2064]
    cq, ckv, kr, zb = w[:, 2064:2576], w[:, 2576:2832], w[:, 2832:2896], w[:, 2896:3920]
    mls, ic, fc = w[:, 3920:6480], w[:, 6480:6488], w[:, 6488:6496]
    pad = jnp.zeros((w.shape[0], N_PROJ - COL_GATES - 32), w.dtype)
    return jnp.concatenate([qa_za, zb, cq, mls, ckv, kr, _rope_swap(kr), ga, ba, ic, fc, pad],
                           axis=1).astype(BF16)


def _prep_w_uq(w):
    w = w.reshape(Q_LORA, H_B, HEAD + ROPE_DIM)
    pe = w[..., HEAD:]
    return jnp.concatenate([w[..., :HEAD], pe, _rope_swap(pe)], axis=-1).reshape(Q_LORA, H_B * QK_PAD).astype(BF16)


def _rope_table(seq):
    t = jnp.arange(seq)
    row = (t // GRID_W).astype(F32)
    colp = (t % GRID_W).astype(F32)
    inv = ROPE_THETA ** (-jnp.arange(0, AXIS_ROPE, 2, dtype=F32) / AXIS_ROPE)
    ar = row[:, None] * inv
    ac = colp[:, None] * inv
    cos = jnp.concatenate([jnp.cos(ar), jnp.cos(ar), jnp.cos(ac), jnp.cos(ac)], axis=-1)
    sin = jnp.concatenate([jnp.sin(ar), jnp.sin(ar), jnp.sin(ac), jnp.sin(ac)], axis=-1)
    return jnp.concatenate([cos, sin], axis=-1)


def _gate_params(first, second, lane0_a, lane0_b, extra=None):
    par = jnp.zeros((8, LANES), F32)
    if first is not None:
        par = par.at[0, lane0_a:lane0_a + 8].set(first.reshape(-1))
    if second is not None:
        par = par.at[0, lane0_b:lane0_b + 8].set(second.reshape(-1))
    if extra is not None:
        par = par.at[1, lane0_a:lane0_a + 8].set(extra.reshape(-1))
    return par


def kernel(x_prompt, x_sample, cache_mla_ckv, cache_mla_krope, state_gdn, state_mlstm_c, state_mlstm_n,
           state_mlstm_m, c, c_ctx, w_ada, b_ada, g_pre, g_post, w_in, gdn_conv, gdn_a_log, gdn_dt_bias,
           gdn_norm, mla_q_norm, mla_kv_norm, mla_w_uq, mla_w_ukv, mlstm_b_i, mlstm_b_f, mlstm_norm,
           w_out):
    bp, lp, _ = x_prompt.shape
    bs, ls, _ = x_sample.shape
    past = cache_mla_ckv.shape[2]

    cond = jnp.concatenate([c, c_ctx[None, :], jnp.zeros((16 - bs - 1, D_MODEL), F32)], axis=0)
    mod = _mod_call(cond, w_ada, b_ada)
    cs_lat = _rope_table(ls)
    cs_ctx = jnp.concatenate([jnp.ones((lp, ROPE_DIM), F32), jnp.zeros((lp, ROPE_DIM), F32)], axis=-1)
    kr_cache = jnp.pad(cache_mla_krope, ((0, 0), (0, 0), (0, 0), (0, LANES - ROPE_DIM)))

    xp = x_prompt.reshape(1, bp * lp, D_MODEL)
    xs = x_sample
    new_ckv, new_kr, new_gdn, new_c, new_n, new_m = [], [], [], [], [], []
    for l in range(DEPTH):
        w_in_l = _prep_w_in(w_in[l])
        wq = _prep_w_uq(mla_w_uq[l])
        wkv = mla_w_ukv[l].astype(BF16)
        w_out_l = w_out[l].astype(BF16)
        mod_l = mod[l].reshape(16, 1, 3 * D_MODEL)
        gdn_par = _gate_params(gdn_dt_bias[l], None, 0, 8, extra=gdn_a_log[l])
        mls_par = _gate_params(mlstm_b_i[l], mlstm_b_f[l], 16, 24)
        g_pre_l = g_pre[l][None, :]
        g_post_l = g_post[l][None, :]
        gdn_norm_l = gdn_norm[l][None, :]
        mls_norm_l = mlstm_norm[l][None, :]
        q_norm_l = mla_q_norm[l][None, :]
        kv_norm_l = mla_kv_norm[l][None, :]

        proj = _inproj_call(xp, g_pre_l, mod_l, bs, w_in_l).reshape(bp, lp, N_PROJ)
        mix_a, s_gdn = _gdn_call(proj, gdn_conv[l], gdn_par, gdn_norm_l, None, True)
        q, k, v, ckvn, kr = _mla_prep_call(proj, cs_ctx, q_norm_l, kv_norm_l, wq, wkv, True)
        mix_b = _attn_call(q, [(k, v)], proj)
        mix_c, cm, nm, mm = _mlstm_call(proj, mls_par, mls_norm_l, None, True)
        xp = _outproj_call(mix_a.reshape(1, bp * lp, W_A), mix_b.reshape(1, bp * lp, W_B),
                           mix_c.reshape(1, bp * lp, W_C), w_out_l, xp, mod_l, bs, g_post_l)
        new_ckv.append(ckvn)
        new_kr.append(kr)
        new_gdn.append(s_gdn)
        new_c.append(cm)
        new_n.append(nm[:, :, :, 0, :])
        new_m.append(mm[:, :, :, 0, 0])

        proj = _inproj_call(xs, g_pre_l, mod_l, 0, w_in_l)
        (mix_a,) = _gdn_call(proj, gdn_conv[l], gdn_par, gdn_norm_l, state_gdn[:, l], False)
        q, k, v = _mla_prep_call(proj, cs_lat, q_norm_l, kv_norm_l, wq, wkv, False)
        k_ctx, v_ctx = _kv_cache_call(cache_mla_ckv[:, l], kr_cache[:, l], wkv)
        mix_b = _attn_call(q, [(k_ctx, v_ctx), (k, v)], proj)
        m0 = jnp.broadcast_to(state_mlstm_m[:, l, :, :, None, None], (bs, 2, H_C, 1, LANES))
        (mix_c,) = _mlstm_call(proj, mls_par, mls_norm_l,
                               (state_mlstm_c[:, l], state_mlstm_n[:, l][..., None], m0), False)
        xs = _outproj_call(mix_a, mix_b, mix_c, w_out_l, xs, mod_l, 0, g_post_l)

    return (xp.reshape(bp, lp, D_MODEL), xs,
            jnp.stack(new_ckv, axis=1), jnp.stack(new_kr, axis=1), jnp.stack(new_gdn, axis=1),
            jnp.stack(new_c, axis=1), jnp.stack(new_n, axis=1), jnp.stack(new_m, axis=1))
```

```python
import functools
import math

import jax
import jax.numpy as jnp
from jax import lax
from jax.experimental import pallas as pl
from jax.experimental.pallas import tpu as pltpu

F32 = jnp.float32
BF16 = jnp.bfloat16

D_MODEL = 2048
DEPTH = 2
GRID_W = 64
EPS = 1e-6
CHUNK = 64
H_A = 4
H_B = 8
H_C = 4
HEAD = 128
Q_LORA = 512
KV_LORA = 256
ROPE_DIM = 64
AXIS_ROPE = ROPE_DIM // 2
ROPE_THETA = 10000.0
SM_SCALE = (HEAD + ROPE_DIM) ** -0.5
Q_SCALE = SM_SCALE * math.log2(math.e)
Q_SUB = 256
QK_SCALE = HEAD ** -0.5
W_A = H_A * HEAD
W_B = H_B * HEAD
W_C = H_C * HEAD
LANES = 128
SUBLANES = 8
QK_PAD = 256

COL_GDN = 0
COL_ZB = 2048
COL_CQ = 3072
COL_MLSTM = 3584
COL_CKV = 6144
COL_ROPE = 6400
COL_GATES = 6528
N_PROJ = 6656
VMEM_LIMIT = 56 * 1024 * 1024


def _cparams(sem):
    return pltpu.CompilerParams(dimension_semantics=sem, vmem_limit_bytes=VMEM_LIMIT)


def _tile(n, pref):
    t = min(n, pref)
    while n % t:
        t //= 2
    return t


def _dot(a, b):
    return jnp.dot(a, b, preferred_element_type=F32)


def _bdot(a, b):
    return _dot(a.astype(BF16), b.astype(BF16))


def _dot_nt(a, b):
    return lax.dot_general(a, b, (((1,), (1,)), ((), ())), preferred_element_type=F32)


def _dot_tn(a, b):
    return lax.dot_general(a, b, (((0,), (0,)), ((), ())), preferred_element_type=F32)


def _split2(x):
    hi = x.astype(BF16)
    return hi, (x - hi.astype(F32)).astype(BF16)


def _dot3(a, b):
    return _dot(a[0], b[0]) + (_dot(a[0], b[1]) + _dot(a[1], b[0]))


def _split3(x):
    x1 = x.astype(BF16)
    r1 = x - x1.astype(F32)
    x2 = r1.astype(BF16)
    x3 = (r1 - x2.astype(F32)).astype(BF16)
    return x1, x2, x3


def _dot01_right(x, m01):
    m = m01.astype(BF16)
    x1, x2, x3 = _split3(x)
    return _dot(x1, m) + _dot(x2, m) + _dot(x3, m)


def _dot01_left(m01, x):
    m = m01.astype(BF16)
    x1, x2, x3 = _split3(x)
    return _dot(m, x1) + _dot(m, x2) + _dot(m, x3)


def _rms(x, g):
    return x * lax.rsqrt(jnp.mean(x * x, axis=-1, keepdims=True) + EPS) * g


def _silu(x):
    return x * jax.nn.sigmoid(x)


def _softplus(x):
    return jnp.maximum(x, 0.0) + jnp.log1p(jnp.exp(-jnp.abs(x)))


def _tri_inv_many(mats, r, c):
    eye = (r == c).astype(F32)
    a0s = [_split2(jnp.where((r >> 3) == (c >> 3), a, 0.0)) for a in mats]
    xs = [eye - jnp.where((r >> 3) == (c >> 3), a, 0.0) for a in mats]
    a2s = [_split2(_dot3(s, s)) for s in a0s]
    xs = [x + _dot3(_split2(x), s) for x, s in zip(xs, a2s)]
    a4s = [_split2(_dot3(s, s)) for s in a2s]
    xs = [x + _dot3(_split2(x), s) for x, s in zip(xs, a4s)]
    for sh in (4, 5, 6):
        sel = ((r >> sh) == (c >> sh)) != ((r >> (sh - 1)) == (c >> (sh - 1)))
        offs = [_split2(jnp.where(sel, a, 0.0)) for a in mats]
        xsp = [_split2(x) for x in xs]
        ys = [_split2(_dot3(x, o)) for x, o in zip(xsp, offs)]
        xs = [x - _dot3(y, xp) for x, y, xp in zip(xs, ys, xsp)]
    return xs


def _mod_kernel(c_ref, w_ref, b_ref, o_ref):
    o_ref[0] = _bdot(_silu(c_ref[...]), w_ref[0]) + b_ref[0]


def _mod_call(cond, w_ada, b_ada):
    n = w_ada.shape[-1]
    tn = _tile(n, 1536)
    return pl.pallas_call(
        _mod_kernel,
        out_shape=jax.ShapeDtypeStruct((DEPTH, cond.shape[0], n), F32),
        grid=(DEPTH, n // tn),
        in_specs=[pl.BlockSpec(cond.shape, lambda l, j: (0, 0)),
                  pl.BlockSpec((1, D_MODEL, tn), lambda l, j: (l, 0, j)),
                  pl.BlockSpec((1, 1, tn), lambda l, j: (l, 0, j))],
        out_specs=pl.BlockSpec((1, cond.shape[0], tn), lambda l, j: (l, 0, j)),
        compiler_params=_cparams(("parallel", "parallel")),
        name="adaln_mod",
    )(cond, w_ada, b_ada.reshape(DEPTH, 1, n))


def _inproj_kernel(x_ref, g_ref, sh_ref, sc_ref, w_ref, o_ref, h_scr):
    @pl.when(pl.program_id(2) == 0)
    def _():
        h = _rms(x_ref[0], g_ref[...]) * (1.0 + sc_ref[0]) + sh_ref[0]
        h_scr[...] = h.astype(BF16)

    o_ref[0] = _dot(h_scr[...], w_ref[...])


def _inproj_call(x, g_pre, mod, row0, w):
    b, l, _ = x.shape
    tm = _tile(l, 512)
    tn = N_PROJ // 4
    return pl.pallas_call(
        _inproj_kernel,
        out_shape=jax.ShapeDtypeStruct((b, l, N_PROJ), F32),
        grid=(b, l // tm, N_PROJ // tn),
        in_specs=[pl.BlockSpec((1, tm, D_MODEL), lambda i, j, k: (i, j, 0)),
                  pl.BlockSpec((1, D_MODEL), lambda i, j, k: (0, 0)),
                  pl.BlockSpec((1, 1, D_MODEL), lambda i, j, k: (i + row0, 0, 0)),
                  pl.BlockSpec((1, 1, D_MODEL), lambda i, j, k: (i + row0, 0, 1)),
                  pl.BlockSpec((D_MODEL, tn), lambda i, j, k: (0, k))],
        out_specs=pl.BlockSpec((1, tm, tn), lambda i, j, k: (i, j, k)),
        scratch_shapes=[pltpu.VMEM((tm, D_MODEL), BF16)],
        compiler_params=_cparams(("parallel", "parallel", "arbitrary")),
        name="in_proj",
    )(x, g_pre, mod, mod, w)


def _outproj_kernel(a_ref, b_ref, c_ref, w_ref, x_ref, gate_ref, g_ref, o_ref):
    y = (_dot(a_ref[0], w_ref[0:W_A]) + _dot(b_ref[0], w_ref[W_A:W_A + W_B])
         + _dot(c_ref[0], w_ref[W_A + W_B:]))
    o_ref[0] = x_ref[0] + gate_ref[0] * _rms(y, g_ref[...])


def _outproj_call(mix_a, mix_b, mix_c, w, x, mod, row0, g_post):
    b, l, _ = x.shape
    tm = _tile(l, 512)
    return pl.pallas_call(
        _outproj_kernel,
        out_shape=jax.ShapeDtypeStruct(x.shape, F32),
        grid=(b, l // tm),
        in_specs=[pl.BlockSpec((1, tm, W_A), lambda i, j: (i, j, 0)),
                  pl.BlockSpec((1, tm, W_B), lambda i, j: (i, j, 0)),
                  pl.BlockSpec((1, tm, W_C), lambda i, j: (i, j, 0)),
                  pl.BlockSpec((D_MODEL, D_MODEL), lambda i, j: (0, 0)),
                  pl.BlockSpec((1, tm, D_MODEL), lambda i, j: (i, j, 0)),
                  pl.BlockSpec((1, 1, D_MODEL), lambda i, j: (i + row0, 0, 2)),
                  pl.BlockSpec((1, D_MODEL), lambda i, j: (0, 0))],
        out_specs=pl.BlockSpec((1, tm, D_MODEL), lambda i, j: (i, j, 0)),
        compiler_params=_cparams(("parallel", "parallel")),
        name="out_proj",
    )(mix_a, mix_b, mix_c, w, x, mod, g_post)


def _rope_pair(x, cs):
    y = x * cs
    return y + pltpu.roll(y, ROPE_DIM, axis=1)


def _store_kv(kv, kpe, k_out, v_out):
    kpe = kpe.astype(BF16)
    for h in range(H_B):
        k_out[0, :, h * QK_PAD:h * QK_PAD + HEAD] = kv[:, 2 * h * HEAD:(2 * h + 1) * HEAD].astype(BF16)
        k_out[0, :, h * QK_PAD + HEAD:(h + 1) * QK_PAD] = kpe
        v_out[0, :, 2 * h * HEAD:(2 * h + 1) * HEAD] = kv[:, (2 * h + 1) * HEAD:(2 * h + 2) * HEAD].astype(BF16)
        v_out[0, :, (2 * h + 1) * HEAD:(2 * h + 2) * HEAD] = jnp.ones((kv.shape[0], HEAD), BF16)


def _mla_prep_kernel(cq_ref, ckv_ref, kr_ref, cs_ref, qn_ref, kvn_ref, wq_ref, wkv_ref,
                     q_out, k_out, v_out, *cache_out):
    cs = cs_ref[...]
    q = _bdot(_rms(cq_ref[0], qn_ref[...]), wq_ref[...])
    for h in range(H_B):
        q_out[0, :, h * QK_PAD:h * QK_PAD + HEAD] = (
            q[:, h * QK_PAD:h * QK_PAD + HEAD] * Q_SCALE).astype(BF16)
        q_out[0, :, h * QK_PAD + HEAD:(h + 1) * QK_PAD] = (
            _rope_pair(q[:, h * QK_PAD + HEAD:(h + 1) * QK_PAD], cs) * Q_SCALE).astype(BF16)
    ckvn = _rms(ckv_ref[0], kvn_ref[...])
    kr = kr_ref[0]
    lane = lax.broadcasted_iota(jnp.int32, kr.shape, 1)
    kpe = jnp.where(lane < ROPE_DIM, _rope_pair(kr, cs), 0.0)
    _store_kv(_bdot(ckvn, wkv_ref[...]), kpe, k_out, v_out)
    if cache_out:
        cache_out[0][0] = ckvn
        cache_out[1][0] = kr[:, :ROPE_DIM]


def _mla_prep_call(proj, cs, q_norm, kv_norm, wq, wkv, emit_cache):
    b, l, _ = proj.shape
    tm = _tile(l, 512)
    out_shape = [jax.ShapeDtypeStruct((b, l, H_B * QK_PAD), BF16),
                 jax.ShapeDtypeStruct((b, l, H_B * QK_PAD), BF16),
                 jax.ShapeDtypeStruct((b, l, 2 * H_B * HEAD), BF16)]
    out_specs = [pl.BlockSpec((1, tm, H_B * QK_PAD), lambda i, j: (i, j, 0)),
                 pl.BlockSpec((1, tm, H_B * QK_PAD), lambda i, j: (i, j, 0)),
                 pl.BlockSpec((1, tm, 2 * H_B * HEAD), lambda i, j: (i, j, 0))]
    if emit_cache:
        out_shape += [jax.ShapeDtypeStruct((b, l, KV_LORA), F32),
                      jax.ShapeDtypeStruct((b, l, ROPE_DIM), F32)]
        out_specs += [pl.BlockSpec((1, tm, KV_LORA), lambda i, j: (i, j, 0)),
                      pl.BlockSpec((1, tm, ROPE_DIM), lambda i, j: (i, j, 0))]
    return pl.pallas_call(
        _mla_prep_kernel,
        out_shape=out_shape,
        grid=(b, l // tm),
        in_specs=[pl.BlockSpec((1, tm, Q_LORA), lambda i, j: (i, j, COL_CQ // Q_LORA)),
                  pl.BlockSpec((1, tm, KV_LORA), lambda i, j: (i, j, COL_CKV // KV_LORA)),
                  pl.BlockSpec((1, tm, LANES), lambda i, j: (i, j, COL_ROPE // LANES)),
                  pl.BlockSpec((tm, LANES), lambda i, j: (j, 0)),
                  pl.BlockSpec((1, Q_LORA), lambda i, j: (0, 0)),
                  pl.BlockSpec((1, KV_LORA), lambda i, j: (0, 0)),
                  pl.BlockSpec((Q_LORA, H_B * QK_PAD), lambda i, j: (0, 0)),
                  pl.BlockSpec((KV_LORA, 2 * H_B * HEAD), lambda i, j: (0, 0))],
        out_specs=out_specs,
        compiler_params=_cparams(("parallel", "parallel")),
        name="mla_prep",
    )(proj, proj, proj, cs, q_norm, kv_norm, wq, wkv)


def _kv_cache_kernel(ckv_ref, kr_ref, wkv_ref, k_out, v_out):
    _store_kv(_bdot(ckv_ref[0], wkv_ref[...]), kr_ref[0], k_out, v_out)


def _kv_cache_call(ckv, kr_pad, wkv):
    b, p, _ = ckv.shape
    return pl.pallas_call(
        _kv_cache_kernel,
        out_shape=[jax.ShapeDtypeStruct((b, p, H_B * QK_PAD), BF16),
                   jax.ShapeDtypeStruct((b, p, 2 * H_B * HEAD), BF16)],
        grid=(b,),
        in_specs=[pl.BlockSpec((1, p, KV_LORA), lambda i: (i, 0, 0)),
                  pl.BlockSpec((1, p, LANES), lambda i: (i, 0, 0)),
                  pl.BlockSpec((KV_LORA, 2 * H_B * HEAD), lambda i: (0, 0))],
        out_specs=[pl.BlockSpec((1, p, H_B * QK_PAD), lambda i: (i, 0, 0)),
                   pl.BlockSpec((1, p, 2 * H_B * HEAD), lambda i: (i, 0, 0))],
        compiler_params=_cparams(("parallel",)),
        name="mla_cache_kv",
    )(ckv, kr_pad, wkv)


def _attn_kernel(*refs, n_parts):
    q_ref = refs[0]
    kv_refs = refs[1:1 + 2 * n_parts]
    z_ref = refs[1 + 2 * n_parts]
    o_ref = refs[2 + 2 * n_parts]
    sub = min(Q_SUB, q_ref.shape[1])
    n_sub = q_ref.shape[1] // sub

    def rows(r):
        return slice(r * sub, (r + 1) * sub)

    def scores(r):
        q = q_ref[0, rows(r), :]
        return [_dot_nt(q, kv_refs[2 * p][0]) for p in range(n_parts)]

    def finish(r, ss):
        m = functools.reduce(jnp.maximum, [jnp.max(s, axis=-1, keepdims=True) for s in ss])
        o2 = functools.reduce(jnp.add, [_dot(jnp.exp2(s - m).astype(BF16), kv_refs[2 * p + 1][0])
                                        for p, s in enumerate(ss)])
        o_ref[0, rows(r), :] = (o2[:, :HEAD] / o2[:, HEAD:] * _silu(z_ref[0, rows(r), :])).astype(BF16)

    ss = scores(0)
    for r in range(n_sub):
        nxt = scores(r + 1) if r + 1 < n_sub else None
        finish(r, ss)
        ss = nxt


def _attn_call(q, kv_parts, proj):
    b, l, _ = q.shape
    tq = _tile(l, 1024)
    in_specs = [pl.BlockSpec((1, tq, QK_PAD), lambda i, h, j: (i, j, h))]
    args = [q]
    for k, v in kv_parts:
        lk = k.shape[1]
        in_specs += [pl.BlockSpec((1, lk, QK_PAD), lambda i, h, j: (i, 0, h)),
                     pl.BlockSpec((1, lk, 2 * HEAD), lambda i, h, j: (i, 0, h))]
        args += [k, v]
    in_specs.append(pl.BlockSpec((1, tq, HEAD), lambda i, h, j: (i, j, COL_ZB // HEAD + h)))
    args.append(proj)
    return pl.pallas_call(
        functools.partial(_attn_kernel, n_parts=len(kv_parts)),
        out_shape=jax.ShapeDtypeStruct((b, l, W_B), BF16),
        grid=(b, H_B, l // tq),
        in_specs=in_specs,
        out_specs=pl.BlockSpec((1, tq, HEAD), lambda i, h, j: (i, j, h)),
        compiler_params=_cparams(("parallel", "parallel", "arbitrary")),
        name="mla_attention",
    )(*args)


def _group(n):
    return 4 if n % 4 == 0 else (2 if n % 2 == 0 else 1)


def _chunk_consts(head, lane0_a, lane0_b, size):
    r = lax.broadcasted_iota(jnp.int32, (size, size), 0)
    c = lax.broadcasted_iota(jnp.int32, (size, size), 1)
    sr = lax.broadcasted_iota(jnp.int32, (LANES, 4 * LANES), 0)
    sc = lax.broadcasted_iota(jnp.int32, (LANES, 4 * LANES), 1)
    grp = sc >> 7
    want = jnp.where((grp & 1) == 0, lane0_a, lane0_b) + 4 * (grp >> 1) + head
    return r, c, (sr == want).astype(F32)


def _running_sums(xf, xb, r, c):
    low = r >= c
    up = r <= c
    pref = _dot01_left(low.astype(F32), jnp.concatenate([xf, xb], axis=1))
    col_f = pref[:, :LANES]
    pref_b = pref[:, LANES:]
    col_b = pref_b[CHUNK - 1:CHUNK, :] - pref_b + xb
    rows = _dot01_left(jnp.ones((CHUNK, CHUNK), F32),
                       jnp.concatenate([jnp.where(up, xf[:, :CHUNK], 0.0),
                                        jnp.where(low, xb[:, :CHUNK], 0.0)], axis=1))
    return col_f, col_b, rows[:, :CHUNK], rows[:, CHUNK:]


def _chunk(j):
    return pl.ds(pl.multiple_of(j * CHUNK, CHUNK), CHUNK)


def _gdn_kernel(*refs, seq, has_init, emit_state):
    q_ref, k_ref, v_ref, z_ref, gt_ref, cwq_ref, cwk_ref, cwv_ref, par_ref, norm_ref = refs[:10]
    pos = 10
    s0_ref = None
    if has_init:
        s0_ref = refs[pos]
        pos += 1
    o_ref = refs[pos]
    pos += 1
    sfin_ref = None
    if emit_state:
        sfin_ref = refs[pos]
        pos += 1
    u_s, wq_s, kd_s, qk_s, egl_s, o_s = refs[pos:pos + 6]

    n = seq // CHUNK
    head = pl.program_id(1)
    group = _group(n)
    r, c, sel = _chunk_consts(head, 0, 8, 2 * CHUNK)
    lane = lax.broadcasted_iota(jnp.int32, (2 * CHUNK, LANES), 1)
    bias = par_ref[0:1, :]
    neg_a = -jnp.exp(par_ref[1:2, :])

    def conv_silu(ref, w_ref, j):
        start = pl.multiple_of(j * CHUNK, CHUNK)
        cur = ref[0, pl.ds(start, CHUNK), :]
        prev = ref[0, pl.ds(jnp.maximum(start - SUBLANES, 0), SUBLANES), :]
        prev = jnp.where(j > 0, prev, 0.0)
        nxt = ref[0, pl.ds(jnp.minimum(start + CHUNK, seq - SUBLANES), SUBLANES), :]
        nxt = jnp.where(j < n - 1, nxt, 0.0)
        win = jnp.concatenate([prev, cur, nxt], axis=0)
        w = w_ref[...]
        y = w[0:1, :] * win[6:6 + CHUNK]
        for t in range(1, 5):
            y = y + w[t:t + 1, :] * win[6 + t:6 + t + CHUNK]
        return _silu(y)

    def l2n(x):
        return x * lax.rsqrt(jnp.sum(x * x, axis=-1, keepdims=True) + EPS)

    top = r < CHUNK
    same = (r >> 6) == (c >> 6)
    rd = jnp.where(top, r, -r)
    cd = jnp.where(top, c, -c)
    incl = same & (rd >= cd)
    strict = same & (rd > cd)
    incl_t = same & (rd <= cd)
    incl_f = incl.astype(F32)
    all_ones = jnp.ones((2 * CHUNK, 2 * CHUNK), F32)

    def pre(jj, carry):
        pairs = [jj * group + p for p in range(group)]
        qs, ks, vs, gts = [], [], [], []
        for a in pairs:
            b = n - 1 - a
            qs.append(jnp.concatenate([l2n(conv_silu(q_ref, cwq_ref, a)),
                                       l2n(conv_silu(q_ref, cwq_ref, b))], axis=0) * QK_SCALE)
            ks.append(jnp.concatenate([l2n(conv_silu(k_ref, cwk_ref, a)),
                                       l2n(conv_silu(k_ref, cwk_ref, b))], axis=0))
            vs.append(jnp.concatenate([conv_silu(v_ref, cwv_ref, a),
                                       conv_silu(v_ref, cwv_ref, b)], axis=0))
            x = jnp.concatenate([gt_ref[0, _chunk(a), :], gt_ref[0, _chunk(b), :]], axis=0) + bias
            gts.append(jnp.where(lane < 8, neg_a * _softplus(x), jax.nn.sigmoid(x)))
        gbs = [_dot01_right(g, sel) for g in gts]
        gs = [jnp.where(top, gb[:, 0:LANES], gb[:, 2 * LANES:3 * LANES]) for gb in gbs]
        betas = [jnp.where(top, gb[:, LANES:2 * LANES], gb[:, 3 * LANES:]) for gb in gbs]
        cols = [_dot01_left(incl_f, g) for g in gs]
        rows = [_dot01_left(all_ones, jnp.where(incl_t, g, 0.0)) for g in gs]
        k16 = [k.astype(BF16) for k in ks]
        kks = [_dot_nt(k, k) for k in k16]
        qks = [_dot_nt(q.astype(BF16), k) for q, k in zip(qs, k16)]
        decays = [jnp.exp(jnp.where(incl, col - row, -jnp.inf)) for col, row in zip(cols, rows)]
        ts = _tri_inv_many([jnp.where(strict, beta * kk * decay, 0.0)
                            for beta, kk, decay in zip(betas, kks, decays)], r, c)
        egs = [jnp.exp(col) for col in cols]
        uws = [_bdot(t, jnp.concatenate([v * beta, k * beta * eg], axis=1))
               for t, v, k, beta, eg in zip(ts, vs, ks, betas, egs)]
        for p, a in enumerate(pairs):
            b = n - 1 - a
            col = cols[p]
            glast = jnp.where(top, col[CHUNK - 1:CHUNK, :], col[CHUNK:CHUNK + 1, :])
            kd = (ks[p] * jnp.exp(glast - col)).astype(BF16)
            qd = (qs[p] * egs[p]).astype(BF16)
            qkd = (qks[p] * decays[p]).astype(BF16)
            egl = jnp.exp(glast)
            for d, ci in ((0, a), (1, b)):
                rs = slice(d * CHUNK, (d + 1) * CHUNK)
                sl = _chunk(ci)
                u_s[d, sl, :] = uws[p][rs, :LANES]
                wq_s[d, ci, 0:CHUNK, :] = uws[p][rs, LANES:].astype(BF16)
                wq_s[d, ci, CHUNK:, :] = qd[rs]
                kd_s[d, sl, :] = kd[rs]
                qk_s[d, sl, :] = qkd[rs, rs]
                egl_s[d, ci] = egl[d * CHUNK:d * CHUNK + SUBLANES]
        return carry

    lax.fori_loop(0, n // group, pre, 0)

    def body(i, carry):
        cis = (i, n - 1 - i)
        sls = [_chunk(ci) for ci in cis]
        wss = [_dot(wq_s[d, cis[d]], carry[d].astype(BF16)) for d in range(2)]
        v_new = [(u_s[d, sls[d], :] - wss[d][:CHUNK]).astype(BF16) for d in range(2)]
        new = [carry[d] * egl_s[d, cis[d]][0:1, :] + _dot_tn(kd_s[d, sls[d], :], v_new[d])
               for d in range(2)]
        for d in range(2):
            o_s[d, sls[d], :] = wss[d][CHUNK:] + _dot(qk_s[d, sls[d], :], v_new[d])
        return tuple(new)

    if has_init:
        init = (s0_ref[0, 0, 0], s0_ref[0, 1, 0])
    else:
        init = (jnp.zeros((HEAD, HEAD), F32), jnp.zeros((HEAD, HEAD), F32))
    s_f, s_b = lax.fori_loop(0, n, body, init)
    if emit_state:
        sfin_ref[0, 0, 0] = s_f
        sfin_ref[0, 1, 0] = s_b

    def post(j, carry):
        sl = _chunk(j)
        o = _rms(o_s[0, sl, :] + o_s[1, sl, :], norm_ref[...])
        o_ref[0, sl, :] = (o * _silu(z_ref[0, sl, :])).astype(BF16)
        return carry

    lax.fori_loop(0, n, post, 0, unroll=2)


def _gdn_call(proj, conv_w, par, norm, s0, emit_state):
    b, l, _ = proj.shape
    n = l // CHUNK
    base = COL_GDN // HEAD

    def col(k):
        return pl.BlockSpec((1, l, HEAD), lambda i, h: (i, 0, base + k * H_A + h))

    in_specs = [col(0), col(1), col(2), col(3),
                pl.BlockSpec((1, l, LANES), lambda i, h: (i, 0, COL_GATES // LANES)),
                pl.BlockSpec((5, HEAD), lambda i, h: (0, h)),
                pl.BlockSpec((5, HEAD), lambda i, h: (0, H_A + h)),
                pl.BlockSpec((5, HEAD), lambda i, h: (0, 2 * H_A + h)),
                pl.BlockSpec((SUBLANES, LANES), lambda i, h: (0, 0)),
                pl.BlockSpec((1, HEAD), lambda i, h: (0, 0))]
    args = [proj, proj, proj, proj, proj, conv_w, conv_w, conv_w, par, norm]
    state_spec = pl.BlockSpec((1, 2, 1, HEAD, HEAD), lambda i, h: (i, 0, h, 0, 0))
    if s0 is not None:
        in_specs.append(state_spec)
        args.append(s0)
    out_shape = [jax.ShapeDtypeStruct((b, l, W_A), BF16)]
    out_specs = [pl.BlockSpec((1, l, HEAD), lambda i, h: (i, 0, h))]
    if emit_state:
        out_shape.append(jax.ShapeDtypeStruct((b, 2, H_A, HEAD, HEAD), F32))
        out_specs.append(state_spec)
    return pl.pallas_call(
        functools.partial(_gdn_kernel, seq=l, has_init=s0 is not None, emit_state=emit_state),
        out_shape=out_shape,
        grid=(b, H_A),
        in_specs=in_specs,
        out_specs=out_specs,
        scratch_shapes=[pltpu.VMEM((2, l, HEAD), F32),
                        pltpu.VMEM((2, n, 2 * CHUNK, HEAD), BF16),
                        pltpu.VMEM((2, l, HEAD), BF16),
                        pltpu.VMEM((2, l, CHUNK), BF16),
                        pltpu.VMEM((2, n, SUBLANES, LANES), F32),
                        pltpu.VMEM((2, l, HEAD), F32)],
        compiler_params=_cparams(("parallel", "parallel")),
        name="gdn",
    )(*args)


def _mlstm_kernel(*refs, seq, has_init, emit_state):
    q_ref, k_ref, v_ref, og_ref, z_ref, gt_ref, par_ref, norm_ref = refs[:8]
    pos = 8
    if has_init:
        c0_ref, n0_ref, m0_ref = refs[pos:pos + 3]
        pos += 3
    o_ref = refs[pos]
    pos += 1
    if emit_state:
        cfin_ref, nfin_ref, mfin_ref = refs[pos:pos + 3]
        pos += 3
    bc_s, mx_s, lw_s, ld_s, qk_s, bl_s, h_s = refs[pos:pos + 7]

    n = seq // CHUNK
    head = pl.program_id(1)
    group = _group(n)
    r, c, sel = _chunk_consts(head, 16, 24, CHUNK)
    lane = lax.broadcasted_iota(jnp.int32, (CHUNK, LANES), 1)
    bias = par_ref[0:1, :]
    eye = r == c
    ones = jnp.ones((CHUNK, LANES), F32)

    def dup(x):
        return jnp.concatenate([x, x], axis=1)

    def pre(jj, carry):
        chunks = [jj * group + p for p in range(group)]
        gts = []
        for j in chunks:
            x = gt_ref[0, _chunk(j), :] + bias
            gts.append(jnp.where(lane < 24, x, -_softplus(-x)))
        gbs = [_dot01_right(g, sel) for g in gts]
        bcss = [_running_sums(gb[:, LANES:2 * LANES], gb[:, 3 * LANES:], r, c) for gb in gbs]
        ipss = [_dot01_left(jnp.ones((CHUNK, CHUNK), F32),
                            jnp.concatenate([jnp.where(eye, gb[:, 0:CHUNK], 0.0),
                                             jnp.where(eye, gb[:, 2 * LANES:2 * LANES + CHUNK], 0.0)],
                                            axis=1)) for gb in gbs]
        for j, gb, bcs, ips in zip(chunks, gbs, bcss, ipss):
            sl = _chunk(j)
            qk_s[sl, :] = _dot_nt((q_ref[0, sl, :] * QK_SCALE).astype(BF16),
                                  k_ref[0, sl, :].astype(BF16)).astype(BF16)
            for d in range(2):
                incl = (r <= c) if d else (r >= c)
                bc = bcs[d]
                ip = gb[:, 2 * d * LANES:(2 * d + 1) * LANES]
                logd = jnp.where(incl, bc[:, :CHUNK] - bcs[2 + d] + ips[:, d * CHUNK:(d + 1) * CHUNK],
                                 -jnp.inf)
                blast = bc[0:1, :] if d else bc[CHUNK - 1:CHUNK, :]
                logw = blast - bc + ip
                bc_s[d, sl, :] = bc
                mx_s[d, sl, :] = jnp.broadcast_to(jnp.max(logd, axis=-1, keepdims=True), (CHUNK, LANES))
                lw_s[d, sl, :] = logw
                ld_s[d, sl, :] = logd
                bl_s[d, j, 0:SUBLANES, :] = jnp.broadcast_to(blast, (SUBLANES, LANES))
                bl_s[d, j, SUBLANES:, :] = jnp.broadcast_to(jnp.max(logw, axis=0, keepdims=True),
                                                            (SUBLANES, LANES))
        return carry

    lax.fori_loop(0, n // group, pre, 0)

    def step(ci, d, cn, m):
        sl = _chunk(ci)
        q = (q_ref[0, sl, :] * QK_SCALE).astype(BF16)
        k = k_ref[0, sl, :]
        v2 = jnp.concatenate([v_ref[0, sl, :], ones], axis=1).astype(BF16)
        m_inter = bc_s[d, sl, :] + m
        m_t = jnp.maximum(m_inter, mx_s[d, sl, :])
        w_inter = jnp.exp(m_inter - m_t)
        sw = qk_s[sl, :].astype(F32) * jnp.exp(ld_s[d, sl, :] - m_t[:, :CHUNK])
        num2 = dup(w_inter) * _dot(q, cn.astype(BF16)) + _dot(sw.astype(BF16), v2)
        h_s[d, sl, :] = num2[:, :LANES] / jnp.maximum(jnp.abs(num2[:, LANES:]), jnp.exp(-m_t))
        blast = bl_s[d, ci, 0:1, :]
        m_new = jnp.maximum(blast + m, bl_s[d, ci, SUBLANES:SUBLANES + 1, :])
        dec = jnp.exp(blast + m - m_new)
        wk = (k * jnp.exp(lw_s[d, sl, :] - m_new)).astype(BF16)
        return dup(dec) * cn + _dot_tn(wk, v2), m_new

    def body(i, carry):
        cn_f, m_f = step(i, 0, carry[0], carry[1])
        cn_b, m_b = step(n - 1 - i, 1, carry[2], carry[3])
        return cn_f, m_f, cn_b, m_b

    if has_init:
        init = []
        for d in range(2):
            nb = jnp.broadcast_to(n0_ref[0, d, 0], (HEAD, LANES))
            init += [jnp.concatenate([c0_ref[0, d, 0], nb], axis=1), m0_ref[0, d, 0]]
        init = tuple(init)
    else:
        init = (jnp.zeros((HEAD, 2 * LANES), F32), jnp.zeros((1, LANES), F32)) * 2
    fin = lax.fori_loop(0, n, body, init)
    if emit_state:
        for d in range(2):
            cn, m = fin[2 * d], fin[2 * d + 1]
            cfin_ref[0, d, 0] = cn[:, :LANES]
            nfin_ref[0, d, 0] = jnp.transpose(cn[:, LANES:])[0:1, :]
            mfin_ref[0, d, 0] = m

    def post(j, carry):
        sl = _chunk(j)
        hh = _rms(h_s[0, sl, :] + h_s[1, sl, :], norm_ref[...])
        o_ref[0, sl, :] = (hh * jax.nn.sigmoid(og_ref[0, sl, :]) * _silu(z_ref[0, sl, :])).astype(BF16)
        return carry

    lax.fori_loop(0, n, post, 0, unroll=2)


def _mlstm_call(proj, par, norm, init, emit_state):
    b, l, _ = proj.shape
    n = l // CHUNK
    base = COL_MLSTM // HEAD

    def col(k):
        return pl.BlockSpec((1, l, HEAD), lambda i, h: (i, 0, base + k * H_C + h))

    in_specs = [col(0), col(1), col(2), col(3), col(4),
                pl.BlockSpec((1, l, LANES), lambda i, h: (i, 0, COL_GATES // LANES)),
                pl.BlockSpec((SUBLANES, LANES), lambda i, h: (0, 0)),
                pl.BlockSpec((1, HEAD), lambda i, h: (0, 0))]
    args = [proj, proj, proj, proj, proj, proj, par, norm]
    c_spec = pl.BlockSpec((1, 2, 1, HEAD, HEAD), lambda i, h: (i, 0, h, 0, 0))
    row_spec = pl.BlockSpec((1, 2, 1, 1, LANES), lambda i, h: (i, 0, h, 0, 0))
    if init is not None:
        in_specs += [c_spec, pl.BlockSpec((1, 2, 1, HEAD, 1), lambda i, h: (i, 0, h, 0, 0)), row_spec]
        args += list(init)
    out_shape = [jax.ShapeDtypeStruct((b, l, W_C), BF16)]
    out_specs = [pl.BlockSpec((1, l, HEAD), lambda i, h: (i, 0, h))]
    if emit_state:
        out_shape += [jax.ShapeDtypeStruct((b, 2, H_C, HEAD, HEAD), F32),
                      jax.ShapeDtypeStruct((b, 2, H_C, 1, LANES), F32),
                      jax.ShapeDtypeStruct((b, 2, H_C, 1, LANES), F32)]
        out_specs += [c_spec, row_spec, row_spec]
    return pl.pallas_call(
        functools.partial(_mlstm_kernel, seq=l, has_init=init is not None, emit_state=emit_state),
        out_shape=out_shape,
        grid=(b, H_C),
        in_specs=in_specs,
        out_specs=out_specs,
        scratch_shapes=[pltpu.VMEM((2, l, LANES), F32),
                        pltpu.VMEM((2, l, LANES), F32),
                        pltpu.VMEM((2, l, LANES), F32),
                        pltpu.VMEM((2, l, CHUNK), F32),
                        pltpu.VMEM((l, CHUNK), BF16),
                        pltpu.VMEM((2, n, 2 * SUBLANES, LANES), F32),
                        pltpu.VMEM((2, l, HEAD), F32)],
        compiler_params=_cparams(("parallel", "parallel")),
        name="mlstm",
    )(*args)


def _rope_swap(w):
    q = AXIS_ROPE // 2
    return jnp.concatenate([-w[..., q:2 * q], w[..., 0:q], -w[..., 3 * q:4 * q], w[..., 2 * q:3 * q]], axis=-1)


def _prep_w_in(w):
    qa_za, ga, ba = w[:, 0:2048], w[:, 2048:2056], w[:, 2056:2064]
    cq, ckv, kr, zb = w[:, 2064:2576], w[:, 2576:2832], w[:, 2832:2896], w[:, 2896:3920]
    mls, ic, fc = w[:, 3920:6480], w[:, 6480:6488], w[:, 6488:6496]
    pad = jnp.zeros((w.shape[0], N_PROJ - COL_GATES - 32), w.dtype)
    return jnp.concatenate([qa_za, zb, cq, mls, ckv, kr, _rope_swap(kr), ga, ba, ic, fc, pad],
                           axis=1).astype(BF16)


def _prep_w_uq(w):
    w = w.reshape(Q_LORA, H_B, HEAD + ROPE_DIM)
    pe = w[..., HEAD:]
    return jnp.concatenate([w[..., :HEAD], pe, _rope_swap(pe)], axis=-1).reshape(Q_LORA, H_B * QK_PAD).astype(BF16)


def _rope_table(seq):
    t = jnp.arange(seq)
    row = (t // GRID_W).astype(F32)
    colp = (t % GRID_W).astype(F32)
    inv = ROPE_THETA ** (-jnp.arange(0, AXIS_ROPE, 2, dtype=F32) / AXIS_ROPE)
    ar = row[:, None] * inv
    ac = colp[:, None] * inv
    cos = jnp.concatenate([jnp.cos(ar), jnp.cos(ar), jnp.cos(ac), jnp.cos(ac)], axis=-1)
    sin = jnp.concatenate([jnp.sin(ar), jnp.sin(ar), jnp.sin(ac), jnp.sin(ac)], axis=-1)
    return jnp.concatenate([cos, sin], axis=-1)


def _gate_params(first, second, lane0_a, lane0_b, extra=None):
    par = jnp.zeros((SUBLANES, LANES), F32)
    if first is not None:
        par = par.at[0, lane0_a:lane0_a + 8].set(first.reshape(-1))
    if second is not None:
        par = par.at[0, lane0_b:lane0_b + 8].set(second.reshape(-1))
    if extra is not None:
        par = par.at[1, lane0_a:lane0_a + 8].set(extra.reshape(-1))
    return par


def kernel(x_prompt, x_sample, cache_mla_ckv, cache_mla_krope, state_gdn, state_mlstm_c, state_mlstm_n,
           state_mlstm_m, c, c_ctx, w_ada, b_ada, g_pre, g_post, w_in, gdn_conv, gdn_a_log, gdn_dt_bias,
           gdn_norm, mla_q_norm, mla_kv_norm, mla_w_uq, mla_w_ukv, mlstm_b_i, mlstm_b_f, mlstm_norm,
           w_out):
    bp, lp, _ = x_prompt.shape
    bs, ls, _ = x_sample.shape

    cond = jnp.concatenate([c, c_ctx[None, :], jnp.zeros((16 - bs - 1, D_MODEL), F32)], axis=0)
    mod = _mod_call(cond, w_ada, b_ada)
    cs_lat = _rope_table(ls)
    cs_ctx = jnp.concatenate([jnp.ones((lp, ROPE_DIM), F32), jnp.zeros((lp, ROPE_DIM), F32)], axis=-1)
    kr_cache = jnp.pad(cache_mla_krope, ((0, 0), (0, 0), (0, 0), (0, LANES - ROPE_DIM)))

    xp = x_prompt.reshape(1, bp * lp, D_MODEL)
    xs = x_sample
    new_ckv, new_kr, new_gdn, new_c, new_n, new_m = [], [], [], [], [], []
    for l in range(DEPTH):
        w_in_l = _prep_w_in(w_in[l])
        wq = _prep_w_uq(mla_w_uq[l])
        wkv = mla_w_ukv[l].astype(BF16)
        w_out_l = w_out[l].astype(BF16)
        mod_l = mod[l].reshape(16, 1, 3 * D_MODEL)
        gdn_par = _gate_params(gdn_dt_bias[l], None, 0, 8, extra=gdn_a_log[l])
        mls_par = _gate_params(mlstm_b_i[l], mlstm_b_f[l], 16, 24)
        g_pre_l = g_pre[l][None, :]
        g_post_l = g_post[l][None, :]
        gdn_norm_l = gdn_norm[l][None, :]
        mls_norm_l = mlstm_norm[l][None, :]
        q_norm_l = mla_q_norm[l][None, :]
        kv_norm_l = mla_kv_norm[l][None, :]

        proj = _inproj_call(xp, g_pre_l, mod_l, bs, w_in_l).reshape(bp, lp, N_PROJ)
        mix_a, s_gdn = _gdn_call(proj, gdn_conv[l], gdn_par, gdn_norm_l, None, True)
        q, k, v, ckvn, kr = _mla_prep_call(proj, cs_ctx, q_norm_l, kv_norm_l, wq, wkv, True)
        mix_b = _attn_call(q, [(k, v)], proj)
        mix_c, cm, nm, mm = _mlstm_call(proj, mls_par, mls_norm_l, None, True)
        xp = _outproj_call(mix_a.reshape(1, bp * lp, W_A), mix_b.reshape(1, bp * lp, W_B),
                           mix_c.reshape(1, bp * lp, W_C), w_out_l, xp, mod_l, bs, g_post_l)
        new_ckv.append(ckvn)
        new_kr.append(kr)
        new_gdn.append(s_gdn)
        new_c.append(cm)
        new_n.append(nm[:, :, :, 0, :])
        new_m.append(mm[:, :, :, 0, 0])

        proj = _inproj_call(xs, g_pre_l, mod_l, 0, w_in_l)
        (mix_a,) = _gdn_call(proj, gdn_conv[l], gdn_par, gdn_norm_l, state_gdn[:, l], False)
        q, k, v = _mla_prep_call(proj, cs_lat, q_norm_l, kv_norm_l, wq, wkv, False)
        k_ctx, v_ctx = _kv_cache_call(cache_mla_ckv[:, l], kr_cache[:, l], wkv)
        mix_b = _attn_call(q, [(k_ctx, v_ctx), (k, v)], proj)
        m0 = jnp.broadcast_to(state_mlstm_m[:, l, :, :, None, None], (bs, 2, H_C, 1, LANES))
        (mix_c,) = _mlstm_call(proj, mls_par, mls_norm_l,
                               (state_mlstm_c[:, l], state_mlstm_n[:, l][..., None], m0), False)
        xs = _outproj_call(mix_a, mix_b, mix_c, w_out_l, xs, mod_l, 0, g_post_l)

    return (xp.reshape(bp, lp, D_MODEL), xs,
            jnp.stack(new_ckv, axis=1), jnp.stack(new_kr, axis=1), jnp.stack(new_gdn, axis=1),
            jnp.stack(new_c, axis=1), jnp.stack(new_n, axis=1), jnp.stack(new_m, axis=1))
```

```python
import functools
import math

import jax
import jax.numpy as jnp
from jax import lax
from jax.experimental import pallas as pl
from jax.experimental.pallas import tpu as pltpu

F32 = jnp.float32
BF16 = jnp.bfloat16

D_MODEL = 2048
DEPTH = 2
GRID_W = 64
EPS = 1e-6
CHUNK = 64
H_A = 4
H_B = 8
H_C = 4
HEAD = 128
Q_LORA = 512
KV_LORA = 256
ROPE_DIM = 64
AXIS_ROPE = ROPE_DIM // 2
ROPE_THETA = 10000.0
SM_SCALE = (HEAD + ROPE_DIM) ** -0.5
Q_SCALE = SM_SCALE * math.log2(math.e)
Q_SUB = 256
QK_SCALE = HEAD ** -0.5
W_A = H_A * HEAD
W_B = H_B * HEAD
W_C = H_C * HEAD
LANES = 128
SUBLANES = 8
QK_PAD = 256

COL_GDN = 0
COL_ZB = 2048
COL_CQ = 3072
COL_MLSTM = 3584
COL_CKV = 6144
COL_ROPE = 6400
COL_GATES = 6528
N_PROJ = 6656
VMEM_LIMIT = 56 * 1024 * 1024


def _cparams(sem):
    return pltpu.CompilerParams(dimension_semantics=sem, vmem_limit_bytes=VMEM_LIMIT)


def _tile(n, pref):
    t = min(n, pref)
    while n % t:
        t //= 2
    return t


def _dot(a, b):
    return jnp.dot(a, b, preferred_element_type=F32)


def _bdot(a, b):
    return _dot(a.astype(BF16), b.astype(BF16))


def _dot_nt(a, b):
    return lax.dot_general(a, b, (((1,), (1,)), ((), ())), preferred_element_type=F32)


def _dot_tn(a, b):
    return lax.dot_general(a, b, (((0,), (0,)), ((), ())), preferred_element_type=F32)


def _split3(x):
    x1 = x.astype(BF16)
    r1 = x - x1.astype(F32)
    x2 = r1.astype(BF16)
    x3 = (r1 - x2.astype(F32)).astype(BF16)
    return x1, x2, x3


def _dot01_right(x, m01):
    m = m01.astype(BF16)
    x1, x2, x3 = _split3(x)
    return _dot(x1, m) + _dot(x2, m) + _dot(x3, m)


def _dot01_left(m01, x):
    m = m01.astype(BF16)
    x1, x2, x3 = _split3(x)
    return _dot(m, x1) + _dot(m, x2) + _dot(m, x3)


def _rms(x, g):
    return x * lax.rsqrt(jnp.mean(x * x, axis=-1, keepdims=True) + EPS) * g


def _silu(x):
    return x * jax.nn.sigmoid(x)


def _softplus(x):
    return jnp.maximum(x, 0.0) + jnp.log1p(jnp.exp(-jnp.abs(x)))


def _tri_inv_many(mats, r, c):
    eye = (r == c).astype(F32)
    a0 = [jnp.where((r >> 3) == (c >> 3), a, 0.0) for a in mats]
    xs = [eye - a for a in a0]
    a0 = [a.astype(BF16) for a in a0]
    a2 = [_dot(a, a).astype(BF16) for a in a0]
    xs = [x + _dot(x.astype(BF16), a) for x, a in zip(xs, a2)]
    a4 = [_dot(a, a).astype(BF16) for a in a2]
    xs = [x + _dot(x.astype(BF16), a) for x, a in zip(xs, a4)]
    for sh in (4, 5, 6):
        sel = ((r >> sh) == (c >> sh)) != ((r >> (sh - 1)) == (c >> (sh - 1)))
        offs = [jnp.where(sel, a, 0.0).astype(BF16) for a in mats]
        xb = [x.astype(BF16) for x in xs]
        ys = [_dot(x, o).astype(BF16) for x, o in zip(xb, offs)]
        xs = [x - _dot(y, x16) for x, y, x16 in zip(xs, ys, xb)]
    return xs


def _mod_kernel(c_ref, w_ref, b_ref, o_ref):
    o_ref[0] = _bdot(_silu(c_ref[...]), w_ref[0]) + b_ref[0]


def _mod_call(cond, w_ada, b_ada):
    n = w_ada.shape[-1]
    tn = _tile(n, 1536)
    return pl.pallas_call(
        _mod_kernel,
        out_shape=jax.ShapeDtypeStruct((DEPTH, cond.shape[0], n), F32),
        grid=(DEPTH, n // tn),
        in_specs=[pl.BlockSpec(cond.shape, lambda l, j: (0, 0)),
                  pl.BlockSpec((1, D_MODEL, tn), lambda l, j: (l, 0, j)),
                  pl.BlockSpec((1, 1, tn), lambda l, j: (l, 0, j))],
        out_specs=pl.BlockSpec((1, cond.shape[0], tn), lambda l, j: (l, 0, j)),
        compiler_params=_cparams(("parallel", "parallel")),
        name="adaln_mod",
    )(cond, w_ada, b_ada.reshape(DEPTH, 1, n))


def _inproj_kernel(x_ref, g_ref, sh_ref, sc_ref, w_ref, o_ref, h_scr):
    @pl.when(pl.program_id(2) == 0)
    def _():
        h = _rms(x_ref[0], g_ref[...]) * (1.0 + sc_ref[0]) + sh_ref[0]
        h_scr[...] = h.astype(BF16)

    o_ref[0] = _dot(h_scr[...], w_ref[...])


def _inproj_call(x, g_pre, mod, row0, w):
    b, l, _ = x.shape
    tm = _tile(l, 512)
    tn = N_PROJ // 4
    return pl.pallas_call(
        _inproj_kernel,
        out_shape=jax.ShapeDtypeStruct((b, l, N_PROJ), F32),
        grid=(b, l // tm, N_PROJ // tn),
        in_specs=[pl.BlockSpec((1, tm, D_MODEL), lambda i, j, k: (i, j, 0)),
                  pl.BlockSpec((1, D_MODEL), lambda i, j, k: (0, 0)),
                  pl.BlockSpec((1, 1, D_MODEL), lambda i, j, k: (i + row0, 0, 0)),
                  pl.BlockSpec((1, 1, D_MODEL), lambda i, j, k: (i + row0, 0, 1)),
                  pl.BlockSpec((D_MODEL, tn), lambda i, j, k: (0, k))],
        out_specs=pl.BlockSpec((1, tm, tn), lambda i, j, k: (i, j, k)),
        scratch_shapes=[pltpu.VMEM((tm, D_MODEL), BF16)],
        compiler_params=_cparams(("parallel", "parallel", "arbitrary")),
        name="in_proj",
    )(x, g_pre, mod, mod, w)


def _outproj_kernel(a_ref, b_ref, c_ref, w_ref, x_ref, gate_ref, g_ref, o_ref):
    y = (_dot(a_ref[0], w_ref[0:W_A]) + _dot(b_ref[0], w_ref[W_A:W_A + W_B])
         + _dot(c_ref[0], w_ref[W_A + W_B:]))
    o_ref[0] = x_ref[0] + gate_ref[0] * _rms(y, g_ref[...])


def _outproj_call(mix_a, mix_b, mix_c, w, x, mod, row0, g_post):
    b, l, _ = x.shape
    tm = _tile(l, 512)
    return pl.pallas_call(
        _outproj_kernel,
        out_shape=jax.ShapeDtypeStruct(x.shape, F32),
        grid=(b, l // tm),
        in_specs=[pl.BlockSpec((1, tm, W_A), lambda i, j: (i, j, 0)),
                  pl.BlockSpec((1, tm, W_B), lambda i, j: (i, j, 0)),
                  pl.BlockSpec((1, tm, W_C), lambda i, j: (i, j, 0)),
                  pl.BlockSpec((D_MODEL, D_MODEL), lambda i, j: (0, 0)),
                  pl.BlockSpec((1, tm, D_MODEL), lambda i, j: (i, j, 0)),
                  pl.BlockSpec((1, 1, D_MODEL), lambda i, j: (i + row0, 0, 2)),
                  pl.BlockSpec((1, D_MODEL), lambda i, j: (0, 0))],
        out_specs=pl.BlockSpec((1, tm, D_MODEL), lambda i, j: (i, j, 0)),
        compiler_params=_cparams(("parallel", "parallel")),
        name="out_proj",
    )(mix_a, mix_b, mix_c, w, x, mod, g_post)


def _rope_pair(x, cs):
    y = x * cs
    return y + pltpu.roll(y, ROPE_DIM, axis=1)


def _store_kv(kv, kpe, k_out, v_out):
    kpe = kpe.astype(BF16)
    for h in range(H_B):
        k_out[0, :, h * QK_PAD:h * QK_PAD + HEAD] = kv[:, 2 * h * HEAD:(2 * h + 1) * HEAD].astype(BF16)
        k_out[0, :, h * QK_PAD + HEAD:(h + 1) * QK_PAD] = kpe
        v_out[0, :, 2 * h * HEAD:(2 * h + 1) * HEAD] = kv[:, (2 * h + 1) * HEAD:(2 * h + 2) * HEAD].astype(BF16)
        v_out[0, :, (2 * h + 1) * HEAD:(2 * h + 2) * HEAD] = jnp.ones((kv.shape[0], HEAD), BF16)


def _mla_prep_kernel(cq_ref, ckv_ref, kr_ref, cs_ref, qn_ref, kvn_ref, wq_ref, wkv_ref,
                     q_out, k_out, v_out, *cache_out):
    cs = cs_ref[...]
    q = _bdot(_rms(cq_ref[0], qn_ref[...]), wq_ref[...])
    for h in range(H_B):
        q_out[0, :, h * QK_PAD:h * QK_PAD + HEAD] = (
            q[:, h * QK_PAD:h * QK_PAD + HEAD] * Q_SCALE).astype(BF16)
        q_out[0, :, h * QK_PAD + HEAD:(h + 1) * QK_PAD] = (
            _rope_pair(q[:, h * QK_PAD + HEAD:(h + 1) * QK_PAD], cs) * Q_SCALE).astype(BF16)
    ckvn = _rms(ckv_ref[0], kvn_ref[...])
    kr = kr_ref[0]
    lane = lax.broadcasted_iota(jnp.int32, kr.shape, 1)
    kpe = jnp.where(lane < ROPE_DIM, _rope_pair(kr, cs), 0.0)
    _store_kv(_bdot(ckvn, wkv_ref[...]), kpe, k_out, v_out)
    if cache_out:
        cache_out[0][0] = ckvn
        cache_out[1][0] = kr[:, :ROPE_DIM]


def _mla_prep_call(proj, cs, q_norm, kv_norm, wq, wkv, emit_cache):
    b, l, _ = proj.shape
    tm = _tile(l, 512)
    out_shape = [jax.ShapeDtypeStruct((b, l, H_B * QK_PAD), BF16),
                 jax.ShapeDtypeStruct((b, l, H_B * QK_PAD), BF16),
                 jax.ShapeDtypeStruct((b, l, 2 * H_B * HEAD), BF16)]
    out_specs = [pl.BlockSpec((1, tm, H_B * QK_PAD), lambda i, j: (i, j, 0)),
                 pl.BlockSpec((1, tm, H_B * QK_PAD), lambda i, j: (i, j, 0)),
                 pl.BlockSpec((1, tm, 2 * H_B * HEAD), lambda i, j: (i, j, 0))]
    if emit_cache:
        out_shape += [jax.ShapeDtypeStruct((b, l, KV_LORA), F32),
                      jax.ShapeDtypeStruct((b, l, ROPE_DIM), F32)]
        out_specs += [pl.BlockSpec((1, tm, KV_LORA), lambda i, j: (i, j, 0)),
                      pl.BlockSpec((1, tm, ROPE_DIM), lambda i, j: (i, j, 0))]
    return pl.pallas_call(
        _mla_prep_kernel,
        out_shape=out_shape,
        grid=(b, l // tm),
        in_specs=[pl.BlockSpec((1, tm, Q_LORA), lambda i, j: (i, j, COL_CQ // Q_LORA)),
                  pl.BlockSpec((1, tm, KV_LORA), lambda i, j: (i, j, COL_CKV // KV_LORA)),
                  pl.BlockSpec((1, tm, LANES), lambda i, j: (i, j, COL_ROPE // LANES)),
                  pl.BlockSpec((tm, LANES), lambda i, j: (j, 0)),
                  pl.BlockSpec((1, Q_LORA), lambda i, j: (0, 0)),
                  pl.BlockSpec((1, KV_LORA), lambda i, j: (0, 0)),
                  pl.BlockSpec((Q_LORA, H_B * QK_PAD), lambda i, j: (0, 0)),
                  pl.BlockSpec((KV_LORA, 2 * H_B * HEAD), lambda i, j: (0, 0))],
        out_specs=out_specs,
        compiler_params=_cparams(("parallel", "parallel")),
        name="mla_prep",
    )(proj, proj, proj, cs, q_norm, kv_norm, wq, wkv)


def _kv_cache_kernel(ckv_ref, kr_ref, wkv_ref, k_out, v_out):
    _store_kv(_bdot(ckv_ref[0], wkv_ref[...]), kr_ref[0], k_out, v_out)


def _kv_cache_call(ckv, kr_pad, wkv):
    b, p, _ = ckv.shape
    return pl.pallas_call(
        _kv_cache_kernel,
        out_shape=[jax.ShapeDtypeStruct((b, p, H_B * QK_PAD), BF16),
                   jax.ShapeDtypeStruct((b, p, 2 * H_B * HEAD), BF16)],
        grid=(b,),
        in_specs=[pl.BlockSpec((1, p, KV_LORA), lambda i: (i, 0, 0)),
                  pl.BlockSpec((1, p, LANES), lambda i: (i, 0, 0)),
                  pl.BlockSpec((KV_LORA, 2 * H_B * HEAD), lambda i: (0, 0))],
        out_specs=[pl.BlockSpec((1, p, H_B * QK_PAD), lambda i: (i, 0, 0)),
                   pl.BlockSpec((1, p, 2 * H_B * HEAD), lambda i: (i, 0, 0))],
        compiler_params=_cparams(("parallel",)),
        name="mla_cache_kv",
    )(ckv, kr_pad, wkv)


def _attn_kernel(*refs, n_parts):
    q_ref = refs[0]
    kv_refs = refs[1:1 + 2 * n_parts]
    z_ref = refs[1 + 2 * n_parts]
    o_ref = refs[2 + 2 * n_parts]
    sub = min(Q_SUB, q_ref.shape[1])
    n_sub = q_ref.shape[1] // sub

    def rows(r):
        return slice(r * sub, (r + 1) * sub)

    def scores(r):
        q = q_ref[0, rows(r), :]
        return [_dot_nt(q, kv_refs[2 * p][0]) for p in range(n_parts)]

    def finish(r, ss):
        m = functools.reduce(jnp.maximum, [jnp.max(s, axis=-1, keepdims=True) for s in ss])
        o2 = functools.reduce(jnp.add, [_dot(jnp.exp2(s - m).astype(BF16), kv_refs[2 * p + 1][0])
                                        for p, s in enumerate(ss)])
        o_ref[0, rows(r), :] = (o2[:, :HEAD] / o2[:, HEAD:] * _silu(z_ref[0, rows(r), :])).astype(BF16)

    ss = scores(0)
    for r in range(n_sub):
        nxt = scores(r + 1) if r + 1 < n_sub else None
        finish(r, ss)
        ss = nxt


def _attn_call(q, kv_parts, proj):
    b, l, _ = q.shape
    tq = _tile(l, 1024)
    in_specs = [pl.BlockSpec((1, tq, QK_PAD), lambda i, h, j: (i, j, h))]
    args = [q]
    for k, v in kv_parts:
        lk = k.shape[1]
        in_specs += [pl.BlockSpec((1, lk, QK_PAD), lambda i, h, j: (i, 0, h)),
                     pl.BlockSpec((1, lk, 2 * HEAD), lambda i, h, j: (i, 0, h))]
        args += [k, v]
    in_specs.append(pl.BlockSpec((1, tq, HEAD), lambda i, h, j: (i, j, COL_ZB // HEAD + h)))
    args.append(proj)
    return pl.pallas_call(
        functools.partial(_attn_kernel, n_parts=len(kv_parts)),
        out_shape=jax.ShapeDtypeStruct((b, l, W_B), BF16),
        grid=(b, H_B, l // tq),
        in_specs=in_specs,
        out_specs=pl.BlockSpec((1, tq, HEAD), lambda i, h, j: (i, j, h)),
        compiler_params=_cparams(("parallel", "parallel", "arbitrary")),
        name="mla_attention",
    )(*args)


def _group(n, most):
    g = most
    while n % g:
        g //= 2
    return g


def _chunk_consts(head, lane0_a, lane0_b, size, n_dirs):
    r = lax.broadcasted_iota(jnp.int32, (size, size), 0)
    c = lax.broadcasted_iota(jnp.int32, (size, size), 1)
    sr = lax.broadcasted_iota(jnp.int32, (LANES, 2 * n_dirs * LANES), 0)
    sc = lax.broadcasted_iota(jnp.int32, (LANES, 2 * n_dirs * LANES), 1)
    grp = sc >> 7
    want = jnp.where((grp & 1) == 0, lane0_a, lane0_b) + 4 * (grp >> 1) + head
    return r, c, (sr == want).astype(F32)


def _running_sums(xf, xb, r, c):
    low = r >= c
    up = r <= c
    pref = _dot01_left(low.astype(F32), jnp.concatenate([xf, xb], axis=1))
    col_f = pref[:, :LANES]
    pref_b = pref[:, LANES:]
    col_b = pref_b[CHUNK - 1:CHUNK, :] - pref_b + xb
    rows = _dot01_left(jnp.ones((CHUNK, CHUNK), F32),
                       jnp.concatenate([jnp.where(up, xf[:, :CHUNK], 0.0),
                                        jnp.where(low, xb[:, :CHUNK], 0.0)], axis=1))
    return col_f, col_b, rows[:, :CHUNK], rows[:, CHUNK:]


def _chunk(j):
    return pl.ds(pl.multiple_of(j * CHUNK, CHUNK), CHUNK)


def _gdn_kernel(*refs, seq, has_init, emit_state):
    q_ref, k_ref, v_ref, z_ref, gt_ref, cwq_ref, cwk_ref, cwv_ref, par_ref, norm_ref = refs[:10]
    pos = 10
    s0_ref = None
    if has_init:
        s0_ref = refs[pos]
        pos += 1
    o_ref = refs[pos]
    pos += 1
    sfin_ref = None
    if emit_state:
        sfin_ref = refs[pos]
        pos += 1
    m_s, ku_s, qu_s, egl_s, o_s = refs[pos:pos + 5]

    n = seq // CHUNK
    head = pl.program_id(1)
    group = _group(n, 8)
    r, c, sel = _chunk_consts(head, 0, 8, 2 * CHUNK, 1)
    lane = lax.broadcasted_iota(jnp.int32, (CHUNK, LANES), 1)
    bias = par_ref[0:1, :]
    neg_a = -jnp.exp(par_ref[1:2, :])

    def conv_silu(ref, w_ref, j):
        start = pl.multiple_of(j * CHUNK, CHUNK)
        cur = ref[0, pl.ds(start, CHUNK), :]
        prev = ref[0, pl.ds(jnp.maximum(start - SUBLANES, 0), SUBLANES), :]
        prev = jnp.where(j > 0, prev, 0.0)
        nxt = ref[0, pl.ds(jnp.minimum(start + CHUNK, seq - SUBLANES), SUBLANES), :]
        nxt = jnp.where(j < n - 1, nxt, 0.0)
        win = jnp.concatenate([prev, cur, nxt], axis=0)
        w = w_ref[...]
        y = w[0:1, :] * win[6:6 + CHUNK]
        for t in range(1, 5):
            y = y + w[t:t + 1, :] * win[6 + t:6 + t + CHUNK]
        return _silu(y)

    def l2n(x):
        return x * lax.rsqrt(jnp.sum(x * x, axis=-1, keepdims=True) + EPS)

    top = r < CHUNK
    same = (r >> 6) == (c >> 6)
    rd = jnp.where(top, r, -r)
    cd = jnp.where(top, c, -c)
    incl = same & (rd >= cd)
    strict = same & (rd > cd)
    incl_t = same & (rd <= cd)
    incl_f = incl.astype(F32)
    all_ones = jnp.ones((2 * CHUNK, 2 * CHUNK), F32)

    def pre(jj, carry):
        pairs = [jj * group + p for p in range(group)]
        qs, ks, vs, gts = [], [], [], []
        for a in pairs:
            q = l2n(conv_silu(q_ref, cwq_ref, a)) * QK_SCALE
            k = l2n(conv_silu(k_ref, cwk_ref, a))
            v = conv_silu(v_ref, cwv_ref, a)
            qs.append(jnp.concatenate([q, q], axis=0))
            ks.append(jnp.concatenate([k, k], axis=0))
            vs.append(jnp.concatenate([v, v], axis=0))
            x = gt_ref[0, _chunk(a), :] + bias
            act = jnp.where(lane < 8, neg_a * _softplus(x), jax.nn.sigmoid(x))
            gts.append(jnp.concatenate([act, pltpu.roll(act, LANES - 4, axis=1)], axis=0))
        gbs = [_dot01_right(g, sel) for g in gts]
        gs = [gb[:, :LANES] for gb in gbs]
        betas = [gb[:, LANES:] for gb in gbs]
        cols = [_dot01_left(incl_f, g) for g in gs]
        rows = [_dot01_left(all_ones, jnp.where(incl_t, g, 0.0)) for g in gs]
        k16 = [k.astype(BF16) for k in ks]
        kks = [_dot_nt(k, k) for k in k16]
        qks = [_dot_nt(q.astype(BF16), k) for q, k in zip(qs, k16)]
        decays = [jnp.exp(jnp.where(incl, col - row, -jnp.inf)) for col, row in zip(cols, rows)]
        ts = _tri_inv_many([jnp.where(strict, beta * kk * decay, 0.0)
                            for beta, kk, decay in zip(betas, kks, decays)], r, c)
        egs = [jnp.exp(col) for col in cols]
        uws = [_bdot(t, jnp.concatenate([v * beta, k * beta * eg], axis=1)).astype(BF16)
               for t, v, k, beta, eg in zip(ts, vs, ks, betas, egs)]
        glasts = [jnp.where(top, col[CHUNK - 1:CHUNK, :], col[CHUNK:CHUNK + 1, :]) for col in cols]
        kds = [(k * jnp.exp(gl - col)).astype(BF16) for k, gl, col in zip(ks, glasts, cols)]
        qkuw = [_dot((qk * decay).astype(BF16), uw) for qk, decay, uw in zip(qks, decays, uws)]
        for p, a in enumerate(pairs):
            qw = (qs[p] * egs[p] - qkuw[p][:, LANES:]).astype(BF16)
            egl = jnp.exp(glasts[p])
            for d in range(2):
                rs = slice(d * CHUNK, (d + 1) * CHUNK)
                kuw = _dot_tn(kds[p][rs], uws[p][rs])
                m_s[d, a, 0:HEAD, :] = kuw[:, LANES:].astype(BF16)
                m_s[d, a, HEAD:, :] = qw[rs]
                ku_s[d, a] = kuw[:, :LANES]
                qu_s[d, _chunk(a), :] = qkuw[p][rs, :LANES]
                egl_s[d, a] = egl[d * CHUNK:d * CHUNK + SUBLANES]
        return carry

    lax.fori_loop(0, n // group, pre, 0)

    def body(i, carry):
        cis = (i, n - 1 - i)
        ms = [_dot(m_s[d, cis[d]], carry[d].astype(BF16)) for d in range(2)]
        for d in range(2):
            sl = _chunk(cis[d])
            o_s[d, sl, :] = ms[d][HEAD:] + qu_s[d, sl, :]
        return tuple(carry[d] * egl_s[d, cis[d]][0:1, :] - ms[d][:HEAD] + ku_s[d, cis[d]]
                     for d in range(2))

    if has_init:
        init = (s0_ref[0, 0, 0], s0_ref[0, 1, 0])
    else:
        init = (jnp.zeros((HEAD, HEAD), F32), jnp.zeros((HEAD, HEAD), F32))
    s_f, s_b = lax.fori_loop(0, n, body, init)
    if emit_state:
        sfin_ref[0, 0, 0] = s_f
        sfin_ref[0, 1, 0] = s_b

    def post(j, carry):
        sl = _chunk(j)
        o = _rms(o_s[0, sl, :] + o_s[1, sl, :], norm_ref[...])
        o_ref[0, sl, :] = (o * _silu(z_ref[0, sl, :])).astype(BF16)
        return carry

    lax.fori_loop(0, n, post, 0, unroll=2)


def _gdn_call(proj, conv_w, par, norm, s0, emit_state):
    b, l, _ = proj.shape
    n = l // CHUNK
    base = COL_GDN // HEAD

    def col(k):
        return pl.BlockSpec((1, l, HEAD), lambda i, h: (i, 0, base + k * H_A + h))

    in_specs = [col(0), col(1), col(2), col(3),
                pl.BlockSpec((1, l, LANES), lambda i, h: (i, 0, COL_GATES // LANES)),
                pl.BlockSpec((5, HEAD), lambda i, h: (0, h)),
                pl.BlockSpec((5, HEAD), lambda i, h: (0, H_A + h)),
                pl.BlockSpec((5, HEAD), lambda i, h: (0, 2 * H_A + h)),
                pl.BlockSpec((SUBLANES, LANES), lambda i, h: (0, 0)),
                pl.BlockSpec((1, HEAD), lambda i, h: (0, 0))]
    args = [proj, proj, proj, proj, proj, conv_w, conv_w, conv_w, par, norm]
    state_spec = pl.BlockSpec((1, 2, 1, HEAD, HEAD), lambda i, h: (i, 0, h, 0, 0))
    if s0 is not None:
        in_specs.append(state_spec)
        args.append(s0)
    out_shape = [jax.ShapeDtypeStruct((b, l, W_A), BF16)]
    out_specs = [pl.BlockSpec((1, l, HEAD), lambda i, h: (i, 0, h))]
    if emit_state:
        out_shape.append(jax.ShapeDtypeStruct((b, 2, H_A, HEAD, HEAD), F32))
        out_specs.append(state_spec)
    return pl.pallas_call(
        functools.partial(_gdn_kernel, seq=l, has_init=s0 is not None, emit_state=emit_state),
        out_shape=out_shape,
        grid=(b, H_A),
        in_specs=in_specs,
        out_specs=out_specs,
        scratch_shapes=[pltpu.VMEM((2, n, HEAD + CHUNK, HEAD), BF16),
                        pltpu.VMEM((2, n, HEAD, HEAD), F32),
                        pltpu.VMEM((2, l, HEAD), F32),
                        pltpu.VMEM((2, n, SUBLANES, LANES), F32),
                        pltpu.VMEM((2, l, HEAD), F32)],
        compiler_params=_cparams(("parallel", "parallel")),
        name="gdn",
    )(*args)


def _mlstm_kernel(*refs, seq, has_init, emit_state):
    q_ref, k_ref, v_ref, og_ref, z_ref, gt_ref, par_ref, norm_ref = refs[:8]
    pos = 8
    if has_init:
        c0_ref, n0_ref, m0_ref = refs[pos:pos + 3]
        pos += 3
    o_ref = refs[pos]
    pos += 1
    if emit_state:
        cfin_ref, nfin_ref, mfin_ref = refs[pos:pos + 3]
        pos += 3
    bc_s, mx_s, lw_s, ld_s, qk_s, bl_s, h_s = refs[pos:pos + 7]

    n = seq // CHUNK
    head = pl.program_id(1)
    group = _group(n, 8)
    r, c, sel = _chunk_consts(head, 16, 24, CHUNK, 2)
    lane = lax.broadcasted_iota(jnp.int32, (CHUNK, LANES), 1)
    bias = par_ref[0:1, :]
    eye = r == c
    ones = jnp.ones((CHUNK, LANES), F32)

    def dup(x):
        return jnp.concatenate([x, x], axis=1)

    def pre(jj, carry):
        chunks = [jj * group + p for p in range(group)]
        gts = []
        for j in chunks:
            x = gt_ref[0, _chunk(j), :] + bias
            gts.append(jnp.where(lane < 24, x, -_softplus(-x)))
        gbs = [_dot01_right(g, sel) for g in gts]
        bcss = [_running_sums(gb[:, LANES:2 * LANES], gb[:, 3 * LANES:], r, c) for gb in gbs]
        ipss = [_dot01_left(jnp.ones((CHUNK, CHUNK), F32),
                            jnp.concatenate([jnp.where(eye, gb[:, 0:CHUNK], 0.0),
                                             jnp.where(eye, gb[:, 2 * LANES:2 * LANES + CHUNK], 0.0)],
                                            axis=1)) for gb in gbs]
        for j, gb, bcs, ips in zip(chunks, gbs, bcss, ipss):
            sl = _chunk(j)
            qk_s[sl, :] = _dot_nt((q_ref[0, sl, :] * QK_SCALE).astype(BF16),
                                  k_ref[0, sl, :].astype(BF16)).astype(BF16)
            for d in range(2):
                incl = (r <= c) if d else (r >= c)
                bc = bcs[d]
                ip = gb[:, 2 * d * LANES:(2 * d + 1) * LANES]
                logd = jnp.where(incl, bc[:, :CHUNK] - bcs[2 + d] + ips[:, d * CHUNK:(d + 1) * CHUNK],
                                 -jnp.inf)
                blast = bc[0:1, :] if d else bc[CHUNK - 1:CHUNK, :]
                logw = blast - bc + ip
                bc_s[d, sl, :] = bc
                mx_s[d, sl, :] = jnp.broadcast_to(jnp.max(logd, axis=-1, keepdims=True), (CHUNK, LANES))
                lw_s[d, sl, :] = logw
                ld_s[d, sl, :] = logd
                bl_s[d, j, 0:SUBLANES, :] = jnp.broadcast_to(blast, (SUBLANES, LANES))
                bl_s[d, j, SUBLANES:, :] = jnp.broadcast_to(jnp.max(logw, axis=0, keepdims=True),
                                                            (SUBLANES, LANES))
        return carry

    lax.fori_loop(0, n // group, pre, 0)

    def body(i, carry):
        cis = (i, n - 1 - i)
        sls = [_chunk(ci) for ci in cis]
        v2s = [jnp.concatenate([v_ref[0, sl, :], ones], axis=1).astype(BF16) for sl in sls]
        upd, decs, m_news = [], [], []
        for d in range(2):
            m = carry[2 * d + 1]
            blast = bl_s[d, cis[d], 0:1, :]
            m_new = jnp.maximum(blast + m, bl_s[d, cis[d], SUBLANES:SUBLANES + 1, :])
            wk = (k_ref[0, sls[d], :] * jnp.exp(lw_s[d, sls[d], :] - m_new)).astype(BF16)
            upd.append(_dot_tn(wk, v2s[d]))
            decs.append(jnp.exp(blast + m - m_new))
            m_news.append(m_new)
        for d in range(2):
            cn, m = carry[2 * d], carry[2 * d + 1]
            sl = sls[d]
            q = (q_ref[0, sl, :] * QK_SCALE).astype(BF16)
            m_inter = bc_s[d, sl, :] + m
            m_t = jnp.maximum(m_inter, mx_s[d, sl, :])
            w_inter = jnp.exp(m_inter - m_t)
            sw = qk_s[sl, :].astype(F32) * jnp.exp(ld_s[d, sl, :] - m_t[:, :CHUNK])
            num2 = dup(w_inter) * _dot(q, cn.astype(BF16)) + _dot(sw.astype(BF16), v2s[d])
            h_s[d, sl, :] = num2[:, :LANES] / jnp.maximum(jnp.abs(num2[:, LANES:]), jnp.exp(-m_t))
        return (dup(decs[0]) * carry[0] + upd[0], m_news[0],
                dup(decs[1]) * carry[2] + upd[1], m_news[1])

    if has_init:
        init = []
        for d in range(2):
            nb = jnp.broadcast_to(n0_ref[0, d, 0], (HEAD, LANES))
            init += [jnp.concatenate([c0_ref[0, d, 0], nb], axis=1), m0_ref[0, d, 0]]
        init = tuple(init)
    else:
        init = (jnp.zeros((HEAD, 2 * LANES), F32), jnp.zeros((1, LANES), F32)) * 2
    fin = lax.fori_loop(0, n, body, init, unroll=4)
    if emit_state:
        for d in range(2):
            cn, m = fin[2 * d], fin[2 * d + 1]
            cfin_ref[0, d, 0] = cn[:, :LANES]
            nfin_ref[0, d, 0] = jnp.transpose(cn[:, LANES:])[0:1, :]
            mfin_ref[0, d, 0] = m

    def post(j, carry):
        sl = _chunk(j)
        hh = _rms(h_s[0, sl, :] + h_s[1, sl, :], norm_ref[...])
        o_ref[0, sl, :] = (hh * jax.nn.sigmoid(og_ref[0, sl, :]) * _silu(z_ref[0, sl, :])).astype(BF16)
        return carry

    lax.fori_loop(0, n, post, 0, unroll=2)


def _mlstm_call(proj, par, norm, init, emit_state):
    b, l, _ = proj.shape
    n = l // CHUNK
    base = COL_MLSTM // HEAD

    def col(k):
        return pl.BlockSpec((1, l, HEAD), lambda i, h: (i, 0, base + k * H_C + h))

    in_specs = [col(0), col(1), col(2), col(3), col(4),
                pl.BlockSpec((1, l, LANES), lambda i, h: (i, 0, COL_GATES // LANES)),
                pl.BlockSpec((SUBLANES, LANES), lambda i, h: (0, 0)),
                pl.BlockSpec((1, HEAD), lambda i, h: (0, 0))]
    args = [proj, proj, proj, proj, proj, proj, par, norm]
    c_spec = pl.BlockSpec((1, 2, 1, HEAD, HEAD), lambda i, h: (i, 0, h, 0, 0))
    row_spec = pl.BlockSpec((1, 2, 1, 1, LANES), lambda i, h: (i, 0, h, 0, 0))
    if init is not None:
        in_specs += [c_spec, pl.BlockSpec((1, 2, 1, HEAD, 1), lambda i, h: (i, 0, h, 0, 0)), row_spec]
        args += list(init)
    out_shape = [jax.ShapeDtypeStruct((b, l, W_C), BF16)]
    out_specs = [pl.BlockSpec((1, l, HEAD), lambda i, h: (i, 0, h))]
    if emit_state:
        out_shape += [jax.ShapeDtypeStruct((b, 2, H_C, HEAD, HEAD), F32),
                      jax.ShapeDtypeStruct((b, 2, H_C, 1, LANES), F32),
                      jax.ShapeDtypeStruct((b, 2, H_C, 1, LANES), F32)]
        out_specs += [c_spec, row_spec, row_spec]
    return pl.pallas_call(
        functools.partial(_mlstm_kernel, seq=l, has_init=init is not None, emit_state=emit_state),
        out_shape=out_shape,
        grid=(b, H_C),
        in_specs=in_specs,
        out_specs=out_specs,
        scratch_shapes=[pltpu.VMEM((2, l, LANES), F32),
                        pltpu.VMEM((2, l, LANES), F32),
                        pltpu.VMEM((2, l, LANES), F32),
                        pltpu.VMEM((2, l, CHUNK), F32),
                        pltpu.VMEM((l, CHUNK), BF16),
                        pltpu.VMEM((2, n, 2 * SUBLANES, LANES), F32),
                        pltpu.VMEM((2, l, HEAD), F32)],
        compiler_params=_cparams(("parallel", "parallel")),
        name="mlstm",
    )(*args)


def _rope_swap(w):
    q = AXIS_ROPE // 2
    return jnp.concatenate([-w[..., q:2 * q], w[..., 0:q], -w[..., 3 * q:4 * q], w[..., 2 * q:3 * q]], axis=-1)


def _prep_w_in(w):
    qa_za, ga, ba = w[:, 0:2048], w[:, 2048:2056], w[:, 2056:2064]
    cq, ckv, kr, zb = w[:, 2064:2576], w[:, 2576:2832], w[:, 2832:2896], w[:, 2896:3920]
    mls, ic, fc = w[:, 3920:6480], w[:, 6480:6488], w[:, 6488:6496]
    pad = jnp.zeros((w.shape[0], N_PROJ - COL_GATES - 32), w.dtype)
    return jnp.concatenate([qa_za, zb, cq, mls, ckv, kr, _rope_swap(kr), ga, ba, ic, fc, pad],
                           axis=1).astype(BF16)


def _prep_w_uq(w):
    w = w.reshape(Q_LORA, H_B, HEAD + ROPE_DIM)
    pe = w[..., HEAD:]
    return jnp.concatenate([w[..., :HEAD], pe, _rope_swap(pe)], axis=-1).reshape(Q_LORA, H_B * QK_PAD).astype(BF16)


def _rope_table(seq):
    t = jnp.arange(seq)
    row = (t // GRID_W).astype(F32)
    colp = (t % GRID_W).astype(F32)
    inv = ROPE_THETA ** (-jnp.arange(0, AXIS_ROPE, 2, dtype=F32) / AXIS_ROPE)
    ar = row[:, None] * inv
    ac = colp[:, None] * inv
    cos = jnp.concatenate([jnp.cos(ar), jnp.cos(ar), jnp.cos(ac), jnp.cos(ac)], axis=-1)
    sin = jnp.concatenate([jnp.sin(ar), jnp.sin(ar), jnp.sin(ac), jnp.sin(ac)], axis=-1)
    return jnp.concatenate([cos, sin], axis=-1)


def _gate_params(first, second, lane0_a, lane0_b, extra=None):
    par = jnp.zeros((SUBLANES, LANES), F32)
    if first is not None:
        par = par.at[0, lane0_a:lane0_a + 8].set(first.reshape(-1))
    if second is not None:
        par = par.at[0, lane0_b:lane0_b + 8].set(second.reshape(-1))
    if extra is not None:
        par = par.at[1, lane0_a:lane0_a + 8].set(extra.reshape(-1))
    return par


def kernel(x_prompt, x_sample, cache_mla_ckv, cache_mla_krope, state_gdn, state_mlstm_c, state_mlstm_n,
           state_mlstm_m, c, c_ctx, w_ada, b_ada, g_pre, g_post, w_in, gdn_conv, gdn_a_log, gdn_dt_bias,
           gdn_norm, mla_q_norm, mla_kv_norm, mla_w_uq, mla_w_ukv, mlstm_b_i, mlstm_b_f, mlstm_norm,
           w_out):
    bp, lp, _ = x_prompt.shape
    bs, ls, _ = x_sample.shape

    cond = jnp.concatenate([c, c_ctx[None, :], jnp.zeros((16 - bs - 1, D_MODEL), F32)], axis=0)
    mod = _mod_call(cond, w_ada, b_ada)
    cs_lat = _rope_table(ls)
    cs_ctx = jnp.concatenate([jnp.ones((lp, ROPE_DIM), F32), jnp.zeros((lp, ROPE_DIM), F32)], axis=-1)
    kr_cache = jnp.pad(cache_mla_krope, ((0, 0), (0, 0), (0, 0), (0, LANES - ROPE_DIM)))

    xp = x_prompt.reshape(1, bp * lp, D_MODEL)
    xs = x_sample
    new_ckv, new_kr, new_gdn, new_c, new_n, new_m = [], [], [], [], [], []
    for l in range(DEPTH):
        w_in_l = _prep_w_in(w_in[l])
        wq = _prep_w_uq(mla_w_uq[l])
        wkv = mla_w_ukv[l].astype(BF16)
        w_out_l = w_out[l].astype(BF16)
        mod_l = mod[l].reshape(16, 1, 3 * D_MODEL)
        gdn_par = _gate_params(gdn_dt_bias[l], None, 0, 8, extra=gdn_a_log[l])
        mls_par = _gate_params(mlstm_b_i[l], mlstm_b_f[l], 16, 24)
        g_pre_l = g_pre[l][None, :]
        g_post_l = g_post[l][None, :]
        gdn_norm_l = gdn_norm[l][None, :]
        mls_norm_l = mlstm_norm[l][None, :]
        q_norm_l = mla_q_norm[l][None, :]
        kv_norm_l = mla_kv_norm[l][None, :]

        proj = _inproj_call(xp, g_pre_l, mod_l, bs, w_in_l).reshape(bp, lp, N_PROJ)
        mix_a, s_gdn = _gdn_call(proj, gdn_conv[l], gdn_par, gdn_norm_l, None, True)
        q, k, v, ckvn, kr = _mla_prep_call(proj, cs_ctx, q_norm_l, kv_norm_l, wq, wkv, True)
        mix_b = _attn_call(q, [(k, v)], proj)
        mix_c, cm, nm, mm = _mlstm_call(proj, mls_par, mls_norm_l, None, True)
        xp = _outproj_call(mix_a.reshape(1, bp * lp, W_A), mix_b.reshape(1, bp * lp, W_B),
                           mix_c.reshape(1, bp * lp, W_C), w_out_l, xp, mod_l, bs, g_post_l)
        new_ckv.append(ckvn)
        new_kr.append(kr)
        new_gdn.append(s_gdn)
        new_c.append(cm)
        new_n.append(nm[:, :, :, 0, :])
        new_m.append(mm[:, :, :, 0, 0])

        proj = _inproj_call(xs, g_pre_l, mod_l, 0, w_in_l)
        (mix_a,) = _gdn_call(proj, gdn_conv[l], gdn_par, gdn_norm_l, state_gdn[:, l], False)
        q, k, v = _mla_prep_call(proj, cs_lat, q_norm_l, kv_norm_l, wq, wkv, False)
        k_ctx, v_ctx = _kv_cache_call(cache_mla_ckv[:, l], kr_cache[:, l], wkv)
        mix_b = _attn_call(q, [(k_ctx, v_ctx), (k, v)], proj)
        m0 = jnp.broadcast_to(state_mlstm_m[:, l, :, :, None, None], (bs, 2, H_C, 1, LANES))
        (mix_c,) = _mlstm_call(proj, mls_par, mls_norm_l,
                               (state_mlstm_c[:, l], state_mlstm_n[:, l][..., None], m0), False)
        xs = _outproj_call(mix_a, mix_b, mix_c, w_out_l, xs, mod_l, 0, g_post_l)

    return (xp.reshape(bp, lp, D_MODEL), xs,
            jnp.stack(new_ckv, axis=1), jnp.stack(new_kr, axis=1), jnp.stack(new_gdn, axis=1),
            jnp.stack(new_c, axis=1), jnp.stack(new_n, axis=1), jnp.stack(new_m, axis=1))
```

```python
import functools
import math

import jax
import jax.numpy as jnp
from jax import lax
from jax.experimental import pallas as pl
from jax.experimental.pallas import tpu as pltpu

F32 = jnp.float32
BF16 = jnp.bfloat16

D_MODEL = 2048
DEPTH = 2
GRID_W = 64
EPS = 1e-6
CHUNK = 64
H_A = 4
H_B = 8
H_C = 4
HEAD = 128
Q_LORA = 512
KV_LORA = 256
ROPE_DIM = 64
AXIS_ROPE = ROPE_DIM // 2
ROPE_THETA = 10000.0
SM_SCALE = (HEAD + ROPE_DIM) ** -0.5
Q_SCALE = SM_SCALE * math.log2(math.e)
ATTN_AHEAD = 1
Q_SUB = 256
QK_SCALE = HEAD ** -0.5
W_A = H_A * HEAD
W_B = H_B * HEAD
W_C = H_C * HEAD
LANES = 128
SUBLANES = 8
QK_PAD = 256

COL_GDN = 0
COL_ZB = 2048
COL_CQ = 3072
COL_MLSTM = 3584
COL_CKV = 6144
COL_ROPE = 6400
COL_GATES = 6528
N_PROJ = 6656
VMEM_LIMIT = 56 * 1024 * 1024


def _cparams(sem):
    return pltpu.CompilerParams(dimension_semantics=sem, vmem_limit_bytes=VMEM_LIMIT)


def _tile(n, pref):
    t = min(n, pref)
    while n % t:
        t //= 2
    return t


def _dot(a, b):
    return jnp.dot(a, b, preferred_element_type=F32)


def _bdot(a, b):
    return _dot(a.astype(BF16), b.astype(BF16))


def _dot_nt(a, b):
    return lax.dot_general(a, b, (((1,), (1,)), ((), ())), preferred_element_type=F32)


def _dot_tn(a, b):
    return lax.dot_general(a, b, (((0,), (0,)), ((), ())), preferred_element_type=F32)


def _masked_sums(m01, x):
    m = m01.astype(BF16)
    hi = x.astype(BF16)
    lo = (x - hi.astype(F32)).astype(BF16)
    return _dot(m, hi) + _dot(m, lo)


def _pick_lane(x, lane, idx):
    col = jnp.sum(jnp.where(lane == idx, x, 0.0), axis=-1, keepdims=True)
    return jnp.broadcast_to(col, x.shape)


def _rms(x, g):
    return x * lax.rsqrt(jnp.mean(x * x, axis=-1, keepdims=True) + EPS) * g


def _silu(x):
    return x * jax.nn.sigmoid(x)


def _softplus(x):
    return jnp.maximum(x, 0.0) + jnp.log1p(jnp.exp(-jnp.abs(x)))


def _tri_inv_many(mats, r, c):
    eye = (r == c).astype(F32)
    a0 = [jnp.where((r >> 3) == (c >> 3), a, 0.0) for a in mats]
    xs = [eye - a for a in a0]
    a0 = [a.astype(BF16) for a in a0]
    a2 = [_dot(a, a).astype(BF16) for a in a0]
    xs = [x + _dot(x.astype(BF16), a) for x, a in zip(xs, a2)]
    a4 = [_dot(a, a).astype(BF16) for a in a2]
    xs = [x + _dot(x.astype(BF16), a) for x, a in zip(xs, a4)]
    for sh in (4, 5, 6):
        sel = ((r >> sh) == (c >> sh)) != ((r >> (sh - 1)) == (c >> (sh - 1)))
        offs = [jnp.where(sel, a, 0.0).astype(BF16) for a in mats]
        xb = [x.astype(BF16) for x in xs]
        ys = [_dot(x, o).astype(BF16) for x, o in zip(xb, offs)]
        xs = [x - _dot(y, x16) for x, y, x16 in zip(xs, ys, xb)]
    return xs


def _mod_kernel(c_ref, w_ref, b_ref, o_ref):
    o_ref[0] = _bdot(_silu(c_ref[...]), w_ref[0]) + b_ref[0]


def _mod_call(cond, w_ada, b_ada):
    n = w_ada.shape[-1]
    tn = _tile(n, 1536)
    return pl.pallas_call(
        _mod_kernel,
        out_shape=jax.ShapeDtypeStruct((DEPTH, cond.shape[0], n), F32),
        grid=(DEPTH, n // tn),
        in_specs=[pl.BlockSpec(cond.shape, lambda l, j: (0, 0)),
                  pl.BlockSpec((1, D_MODEL, tn), lambda l, j: (l, 0, j)),
                  pl.BlockSpec((1, 1, tn), lambda l, j: (l, 0, j))],
        out_specs=pl.BlockSpec((1, cond.shape[0], tn), lambda l, j: (l, 0, j)),
        compiler_params=_cparams(("parallel", "parallel")),
        name="adaln_mod",
    )(cond, w_ada, b_ada.reshape(DEPTH, 1, n))


def _inproj_kernel(x_ref, g_ref, sh_ref, sc_ref, w_ref, o_ref, h_scr):
    @pl.when(pl.program_id(2) == 0)
    def _():
        h = _rms(x_ref[0], g_ref[...]) * (1.0 + sc_ref[0]) + sh_ref[0]
        h_scr[...] = h.astype(BF16)

    o_ref[0] = _dot(h_scr[...], w_ref[...])


def _inproj_call(x, g_pre, mod, row0, w):
    b, l, _ = x.shape
    tm = _tile(l, 512)
    tn = N_PROJ // 4
    return pl.pallas_call(
        _inproj_kernel,
        out_shape=jax.ShapeDtypeStruct((b, l, N_PROJ), F32),
        grid=(b, l // tm, N_PROJ // tn),
        in_specs=[pl.BlockSpec((1, tm, D_MODEL), lambda i, j, k: (i, j, 0)),
                  pl.BlockSpec((1, D_MODEL), lambda i, j, k: (0, 0)),
                  pl.BlockSpec((1, 1, D_MODEL), lambda i, j, k: (i + row0, 0, 0)),
                  pl.BlockSpec((1, 1, D_MODEL), lambda i, j, k: (i + row0, 0, 1)),
                  pl.BlockSpec((D_MODEL, tn), lambda i, j, k: (0, k))],
        out_specs=pl.BlockSpec((1, tm, tn), lambda i, j, k: (i, j, k)),
        scratch_shapes=[pltpu.VMEM((tm, D_MODEL), BF16)],
        compiler_params=_cparams(("parallel", "parallel", "arbitrary")),
        name="in_proj",
    )(x, g_pre, mod, mod, w)


def _outproj_kernel(a_ref, b_ref, c_ref, w_ref, x_ref, gate_ref, g_ref, o_ref):
    y = (_dot(a_ref[0], w_ref[0:W_A]) + _dot(b_ref[0], w_ref[W_A:W_A + W_B])
         + _dot(c_ref[0], w_ref[W_A + W_B:]))
    o_ref[0] = x_ref[0] + gate_ref[0] * _rms(y, g_ref[...])


def _outproj_call(mix_a, mix_b, mix_c, w, x, mod, row0, g_post):
    b, l, _ = x.shape
    tm = _tile(l, 512)
    return pl.pallas_call(
        _outproj_kernel,
        out_shape=jax.ShapeDtypeStruct(x.shape, F32),
        grid=(b, l // tm),
        in_specs=[pl.BlockSpec((1, tm, W_A), lambda i, j: (i, j, 0)),
                  pl.BlockSpec((1, tm, W_B), lambda i, j: (i, j, 0)),
                  pl.BlockSpec((1, tm, W_C), lambda i, j: (i, j, 0)),
                  pl.BlockSpec((D_MODEL, D_MODEL), lambda i, j: (0, 0)),
                  pl.BlockSpec((1, tm, D_MODEL), lambda i, j: (i, j, 0)),
                  pl.BlockSpec((1, 1, D_MODEL), lambda i, j: (i + row0, 0, 2)),
                  pl.BlockSpec((1, D_MODEL), lambda i, j: (0, 0))],
        out_specs=pl.BlockSpec((1, tm, D_MODEL), lambda i, j: (i, j, 0)),
        compiler_params=_cparams(("parallel", "parallel")),
        name="out_proj",
    )(mix_a, mix_b, mix_c, w, x, mod, g_post)


def _rope_pair(x, cs):
    y = x * cs
    return y + pltpu.roll(y, ROPE_DIM, axis=1)


def _store_kv(kv, kpe, k_out, v_out):
    kpe = kpe.astype(BF16)
    for h in range(H_B):
        k_out[0, :, h * QK_PAD:h * QK_PAD + HEAD] = kv[:, 2 * h * HEAD:(2 * h + 1) * HEAD].astype(BF16)
        k_out[0, :, h * QK_PAD + HEAD:(h + 1) * QK_PAD] = kpe
        v_out[0, :, 2 * h * HEAD:(2 * h + 1) * HEAD] = kv[:, (2 * h + 1) * HEAD:(2 * h + 2) * HEAD].astype(BF16)
        v_out[0, :, (2 * h + 1) * HEAD:(2 * h + 2) * HEAD] = jnp.ones((kv.shape[0], HEAD), BF16)


def _mla_prep_kernel(cq_ref, ckv_ref, kr_ref, cs_ref, qn_ref, kvn_ref, wq_ref, wkv_ref,
                     q_out, k_out, v_out, *cache_out):
    cs = cs_ref[...]
    q = _bdot(_rms(cq_ref[0], qn_ref[...]), wq_ref[...])
    for h in range(H_B):
        q_out[0, :, h * QK_PAD:h * QK_PAD + HEAD] = (
            q[:, h * QK_PAD:h * QK_PAD + HEAD] * Q_SCALE).astype(BF16)
        q_out[0, :, h * QK_PAD + HEAD:(h + 1) * QK_PAD] = (
            _rope_pair(q[:, h * QK_PAD + HEAD:(h + 1) * QK_PAD], cs) * Q_SCALE).astype(BF16)
    ckvn = _rms(ckv_ref[0], kvn_ref[...])
    kr = kr_ref[0]
    lane = lax.broadcasted_iota(jnp.int32, kr.shape, 1)
    kpe = jnp.where(lane < ROPE_DIM, _rope_pair(kr, cs), 0.0)
    _store_kv(_bdot(ckvn, wkv_ref[...]), kpe, k_out, v_out)
    if cache_out:
        cache_out[0][0] = ckvn
        cache_out[1][0] = kr[:, :ROPE_DIM]


def _mla_prep_call(proj, cs, q_norm, kv_norm, wq, wkv, emit_cache):
    b, l, _ = proj.shape
    tm = _tile(l, 512)
    out_shape = [jax.ShapeDtypeStruct((b, l, H_B * QK_PAD), BF16),
                 jax.ShapeDtypeStruct((b, l, H_B * QK_PAD), BF16),
                 jax.ShapeDtypeStruct((b, l, 2 * H_B * HEAD), BF16)]
    out_specs = [pl.BlockSpec((1, tm, H_B * QK_PAD), lambda i, j: (i, j, 0)),
                 pl.BlockSpec((1, tm, H_B * QK_PAD), lambda i, j: (i, j, 0)),
                 pl.BlockSpec((1, tm, 2 * H_B * HEAD), lambda i, j: (i, j, 0))]
    if emit_cache:
        out_shape += [jax.ShapeDtypeStruct((b, l, KV_LORA), F32),
                      jax.ShapeDtypeStruct((b, l, ROPE_DIM), F32)]
        out_specs += [pl.BlockSpec((1, tm, KV_LORA), lambda i, j: (i, j, 0)),
                      pl.BlockSpec((1, tm, ROPE_DIM), lambda i, j: (i, j, 0))]
    return pl.pallas_call(
        _mla_prep_kernel,
        out_shape=out_shape,
        grid=(b, l // tm),
        in_specs=[pl.BlockSpec((1, tm, Q_LORA), lambda i, j: (i, j, COL_CQ // Q_LORA)),
                  pl.BlockSpec((1, tm, KV_LORA), lambda i, j: (i, j, COL_CKV // KV_LORA)),
                  pl.BlockSpec((1, tm, LANES), lambda i, j: (i, j, COL_ROPE // LANES)),
                  pl.BlockSpec((tm, LANES), lambda i, j: (j, 0)),
                  pl.BlockSpec((1, Q_LORA), lambda i, j: (0, 0)),
                  pl.BlockSpec((1, KV_LORA), lambda i, j: (0, 0)),
                  pl.BlockSpec((Q_LORA, H_B * QK_PAD), lambda i, j: (0, 0)),
                  pl.BlockSpec((KV_LORA, 2 * H_B * HEAD), lambda i, j: (0, 0))],
        out_specs=out_specs,
        compiler_params=_cparams(("parallel", "parallel")),
        name="mla_prep",
    )(proj, proj, proj, cs, q_norm, kv_norm, wq, wkv)


def _kv_cache_kernel(ckv_ref, kr_ref, wkv_ref, k_out, v_out):
    _store_kv(_bdot(ckv_ref[0], wkv_ref[...]), kr_ref[0], k_out, v_out)


def _kv_cache_call(ckv, kr_pad, wkv):
    b, p, _ = ckv.shape
    return pl.pallas_call(
        _kv_cache_kernel,
        out_shape=[jax.ShapeDtypeStruct((b, p, H_B * QK_PAD), BF16),
                   jax.ShapeDtypeStruct((b, p, 2 * H_B * HEAD), BF16)],
        grid=(b,),
        in_specs=[pl.BlockSpec((1, p, KV_LORA), lambda i: (i, 0, 0)),
                  pl.BlockSpec((1, p, LANES), lambda i: (i, 0, 0)),
                  pl.BlockSpec((KV_LORA, 2 * H_B * HEAD), lambda i: (0, 0))],
        out_specs=[pl.BlockSpec((1, p, H_B * QK_PAD), lambda i: (i, 0, 0)),
                   pl.BlockSpec((1, p, 2 * H_B * HEAD), lambda i: (i, 0, 0))],
        compiler_params=_cparams(("parallel",)),
        name="mla_cache_kv",
    )(ckv, kr_pad, wkv)


def _attn_kernel(*refs, n_parts):
    q_ref = refs[0]
    kv_refs = refs[1:1 + 2 * n_parts]
    z_ref = refs[1 + 2 * n_parts]
    o_ref = refs[2 + 2 * n_parts]
    sub = min(Q_SUB, q_ref.shape[1])
    n_sub = q_ref.shape[1] // sub

    def rows(r):
        return slice(r * sub, (r + 1) * sub)

    def scores(r):
        q = q_ref[0, rows(r), :]
        return [_dot_nt(q, kv_refs[2 * p][0]) for p in range(n_parts)]

    def finish(r, ss):
        m = functools.reduce(jnp.maximum, [jnp.max(s, axis=-1, keepdims=True) for s in ss])
        o2 = functools.reduce(jnp.add, [_dot(jnp.exp2(s - m).astype(BF16), kv_refs[2 * p + 1][0])
                                        for p, s in enumerate(ss)])
        o_ref[0, rows(r), :] = (o2[:, :HEAD] / o2[:, HEAD:] * _silu(z_ref[0, rows(r), :])).astype(BF16)

    pending = [scores(r) for r in range(min(ATTN_AHEAD, n_sub))]
    for r in range(n_sub):
        if r + ATTN_AHEAD < n_sub:
            pending.append(scores(r + ATTN_AHEAD))
        finish(r, pending.pop(0))


def _attn_call(q, kv_parts, proj):
    b, l, _ = q.shape
    tq = _tile(l, 1024)
    in_specs = [pl.BlockSpec((1, tq, QK_PAD), lambda i, h, j: (i, j, h))]
    args = [q]
    for k, v in kv_parts:
        lk = k.shape[1]
        in_specs += [pl.BlockSpec((1, lk, QK_PAD), lambda i, h, j: (i, 0, h)),
                     pl.BlockSpec((1, lk, 2 * HEAD), lambda i, h, j: (i, 0, h))]
        args += [k, v]
    in_specs.append(pl.BlockSpec((1, tq, HEAD), lambda i, h, j: (i, j, COL_ZB // HEAD + h)))
    args.append(proj)
    return pl.pallas_call(
        functools.partial(_attn_kernel, n_parts=len(kv_parts)),
        out_shape=jax.ShapeDtypeStruct((b, l, W_B), BF16),
        grid=(b, H_B, l // tq),
        in_specs=in_specs,
        out_specs=pl.BlockSpec((1, tq, HEAD), lambda i, h, j: (i, j, h)),
        compiler_params=_cparams(("parallel", "parallel", "arbitrary")),
        name="mla_attention",
    )(*args)


def _group(n, most):
    g = most
    while n % g:
        g //= 2
    return g


def _iotas(size):
    return (lax.broadcasted_iota(jnp.int32, (size, size), 0),
            lax.broadcasted_iota(jnp.int32, (size, size), 1))


def _chunk(j):
    return pl.ds(pl.multiple_of(j * CHUNK, CHUNK), CHUNK)


def _gdn_kernel(*refs, seq, nb, has_init, emit_state):
    q_ref, k_ref, v_ref, z_ref, gt_ref, cwq_ref, cwk_ref, cwv_ref, par_ref, norm_ref = refs[:10]
    pos = 10
    s0_ref = None
    if has_init:
        s0_ref = refs[pos]
        pos += 1
    o_ref = refs[pos]
    pos += 1
    sfin_ref = None
    if emit_state:
        sfin_ref = refs[pos]
        pos += 1
    m_s, ku_s, qu_s, egl_s, o_s = refs[pos:pos + 5]

    n = seq // CHUNK
    head = pl.program_id(1)
    group = _group(nb * n, 8)
    r, c = _iotas(2 * CHUNK)

    def where(a):
        return (0, a) if nb == 1 else divmod(a, n)

    lane = lax.broadcasted_iota(jnp.int32, (CHUNK, LANES), 1)
    bias = par_ref[0:1, :]
    neg_a = -jnp.exp(par_ref[1:2, :])

    def conv_silu(ref, w_ref, a):
        s, j = where(a)
        start = pl.multiple_of(j * CHUNK, CHUNK)
        cur = ref[s, pl.ds(start, CHUNK), :]
        prev = ref[s, pl.ds(jnp.maximum(start - SUBLANES, 0), SUBLANES), :]
        prev = jnp.where(j > 0, prev, 0.0)
        nxt = ref[s, pl.ds(jnp.minimum(start + CHUNK, seq - SUBLANES), SUBLANES), :]
        nxt = jnp.where(j < n - 1, nxt, 0.0)
        win = jnp.concatenate([prev, cur, nxt], axis=0)
        w = w_ref[...]
        y = w[0:1, :] * win[6:6 + CHUNK]
        for t in range(1, 5):
            y = y + w[t:t + 1, :] * win[6 + t:6 + t + CHUNK]
        return _silu(y)

    def l2n(x):
        return x * lax.rsqrt(jnp.sum(x * x, axis=-1, keepdims=True) + EPS)

    top = r < CHUNK
    same = (r >> 6) == (c >> 6)
    rd = jnp.where(top, r, -r)
    cd = jnp.where(top, c, -c)
    incl = same & (rd >= cd)
    strict = same & (rd > cd)
    incl_f = incl.astype(F32)

    def pre(jj, carry):
        pairs = [jj * group + p for p in range(group)]
        qs, ks, vs, gs, betas = [], [], [], [], []
        for a in pairs:
            q = l2n(conv_silu(q_ref, cwq_ref, a)) * QK_SCALE
            k = l2n(conv_silu(k_ref, cwk_ref, a))
            v = conv_silu(v_ref, cwv_ref, a)
            qs.append(jnp.concatenate([q, q], axis=0))
            ks.append(jnp.concatenate([k, k], axis=0))
            vs.append(jnp.concatenate([v, v], axis=0))
            s, j = where(a)
            x = gt_ref[s, _chunk(j), :] + bias
            act = jnp.where(lane < 8, neg_a * _softplus(x), jax.nn.sigmoid(x))
            gs.append(jnp.concatenate([_pick_lane(act, lane, head),
                                       _pick_lane(act, lane, 4 + head)], axis=0))
            betas.append(jnp.concatenate([_pick_lane(act, lane, 8 + head),
                                          _pick_lane(act, lane, 12 + head)], axis=0))
        cols = [_masked_sums(incl_f, g) for g in gs]
        rows = [jnp.transpose(col) for col in cols]
        k16 = [k.astype(BF16) for k in ks]
        kks = [_dot_nt(k, k) for k in k16]
        qks = [_dot_nt(q.astype(BF16), k) for q, k in zip(qs, k16)]
        decays = [jnp.exp(jnp.where(incl, col - row, -jnp.inf)) for col, row in zip(cols, rows)]
        ts = _tri_inv_many([jnp.where(strict, beta * kk * decay, 0.0)
                            for beta, kk, decay in zip(betas, kks, decays)], r, c)
        egs = [jnp.exp(col) for col in cols]
        uws = [_bdot(t, jnp.concatenate([v * beta, k * beta * eg], axis=1)).astype(BF16)
               for t, v, k, beta, eg in zip(ts, vs, ks, betas, egs)]
        glasts = [jnp.where(top, col[CHUNK - 1:CHUNK, :], col[CHUNK:CHUNK + 1, :]) for col in cols]
        kds = [(k * jnp.exp(gl - col)).astype(BF16) for k, gl, col in zip(ks, glasts, cols)]
        qkuw = [_dot((qk * decay).astype(BF16), uw) for qk, decay, uw in zip(qks, decays, uws)]
        for p, a in enumerate(pairs):
            qw = (qs[p] * egs[p] - qkuw[p][:, LANES:]).astype(BF16)
            egl = jnp.exp(glasts[p])
            for d in range(2):
                rs = slice(d * CHUNK, (d + 1) * CHUNK)
                kuw = _dot_tn(kds[p][rs], uws[p][rs])
                m_s[d, a, 0:HEAD, :] = kuw[:, LANES:].astype(BF16)
                m_s[d, a, HEAD:, :] = qw[rs]
                ku_s[d, a] = kuw[:, :LANES]
                qu_s[d, _chunk(a), :] = qkuw[p][rs, :LANES]
                egl_s[d, a] = egl[d * CHUNK:d * CHUNK + SUBLANES]
        return carry

    if nb == 1:
        lax.fori_loop(0, n // group, pre, 0)
    else:
        pre(0, 0)

    chains = [(s, d) for s in range(nb) for d in range(2)]

    def body(i, carry):
        cis = [s * n + (n - 1 - i if d else i) for s, d in chains]
        ms = [_dot(m_s[d, ci], st.astype(BF16)) for (s, d), ci, st in zip(chains, cis, carry)]
        for (s, d), ci, m in zip(chains, cis, ms):
            sl = _chunk(ci)
            o_s[d, sl, :] = m[HEAD:] + qu_s[d, sl, :]
        return tuple(st * egl_s[d, ci][0:1, :] - m[:HEAD] + ku_s[d, ci]
                     for (s, d), ci, st, m in zip(chains, cis, carry, ms))

    if has_init:
        init = tuple(s0_ref[s, d, 0] for s, d in chains)
    else:
        init = tuple(jnp.zeros((HEAD, HEAD), F32) for _ in chains)
    fin = lax.fori_loop(0, n, body, init)
    if emit_state:
        for (s, d), st in zip(chains, fin):
            sfin_ref[s, d, 0] = st

    for s in range(nb):
        def post(j, carry, s=s):
            sl = _chunk(s * n + j)
            o = _rms(o_s[0, sl, :] + o_s[1, sl, :], norm_ref[...])
            o_ref[s, _chunk(j), :] = (o * _silu(z_ref[s, _chunk(j), :])).astype(BF16)
            return carry

        lax.fori_loop(0, n, post, 0, unroll=2)


def _seqs_per_step(b, n):
    nb = 1
    while 2 * nb * n <= 8 and b % (2 * nb) == 0:
        nb *= 2
    return nb


def _gdn_call(proj, conv_w, par, norm, s0, emit_state):
    b, l, _ = proj.shape
    n = l // CHUNK
    nb = _seqs_per_step(b, n)
    base = COL_GDN // HEAD

    def col(k):
        return pl.BlockSpec((nb, l, HEAD), lambda i, h: (i, 0, base + k * H_A + h))

    in_specs = [col(0), col(1), col(2), col(3),
                pl.BlockSpec((nb, l, LANES), lambda i, h: (i, 0, COL_GATES // LANES)),
                pl.BlockSpec((5, HEAD), lambda i, h: (0, h)),
                pl.BlockSpec((5, HEAD), lambda i, h: (0, H_A + h)),
                pl.BlockSpec((5, HEAD), lambda i, h: (0, 2 * H_A + h)),
                pl.BlockSpec((SUBLANES, LANES), lambda i, h: (0, 0)),
                pl.BlockSpec((1, HEAD), lambda i, h: (0, 0))]
    args = [proj, proj, proj, proj, proj, conv_w, conv_w, conv_w, par, norm]
    state_spec = pl.BlockSpec((nb, 2, 1, HEAD, HEAD), lambda i, h: (i, 0, h, 0, 0))
    if s0 is not None:
        in_specs.append(state_spec)
        args.append(s0)
    out_shape = [jax.ShapeDtypeStruct((b, l, W_A), BF16)]
    out_specs = [pl.BlockSpec((nb, l, HEAD), lambda i, h: (i, 0, h))]
    if emit_state:
        out_shape.append(jax.ShapeDtypeStruct((b, 2, H_A, HEAD, HEAD), F32))
        out_specs.append(state_spec)
    return pl.pallas_call(
        functools.partial(_gdn_kernel, seq=l, nb=nb, has_init=s0 is not None, emit_state=emit_state),
        out_shape=out_shape,
        grid=(b // nb, H_A),
        in_specs=in_specs,
        out_specs=out_specs,
        scratch_shapes=[pltpu.VMEM((2, nb * n, HEAD + CHUNK, HEAD), BF16),
                        pltpu.VMEM((2, nb * n, HEAD, HEAD), F32),
                        pltpu.VMEM((2, nb * l, HEAD), F32),
                        pltpu.VMEM((2, nb * n, SUBLANES, LANES), F32),
                        pltpu.VMEM((2, nb * l, HEAD), F32)],
        compiler_params=_cparams(("parallel", "parallel")),
        name="gdn",
    )(*args)


def _mlstm_kernel(*refs, seq, nb, has_init, emit_state):
    q_ref, k_ref, v_ref, og_ref, z_ref, gt_ref, par_ref, norm_ref = refs[:8]
    pos = 8
    if has_init:
        c0_ref, n0_ref, m0_ref = refs[pos:pos + 3]
        pos += 3
    o_ref = refs[pos]
    pos += 1
    if emit_state:
        cfin_ref, nfin_ref, mfin_ref = refs[pos:pos + 3]
        pos += 3
    bc_s, mx_s, lw_s, ld_s, qk_s, bl_s, h_s = refs[pos:pos + 7]

    n = seq // CHUNK
    head = pl.program_id(1)
    group = _group(nb * n, 8)
    r, c = _iotas(CHUNK)

    def where(a):
        return (0, a) if nb == 1 else divmod(a, n)

    lane = lax.broadcasted_iota(jnp.int32, (CHUNK, LANES), 1)
    bias = par_ref[0:1, :]
    low = (r >= c).astype(F32)
    ones = jnp.ones((CHUNK, LANES), F32)

    def dup(x):
        return jnp.concatenate([x, x], axis=1)

    def pre(jj, carry):
        chunks = [jj * group + p for p in range(group)]
        ipss, lfss = [], []
        for a in chunks:
            s, j = where(a)
            x = gt_ref[s, _chunk(j), :] + bias
            act = jnp.where(lane < 24, x, -_softplus(-x))
            ipss.append([_pick_lane(act, lane, 16 + 4 * d + head) for d in range(2)])
            lfss.append([_pick_lane(act, lane, 24 + 4 * d + head) for d in range(2)])
        prefs = [_masked_sums(low, jnp.concatenate(lfs, axis=1)) for lfs in lfss]
        for a, ips, lfs, pref in zip(chunks, ipss, lfss, prefs):
            s, j = where(a)
            sl = _chunk(a)
            qk_s[sl, :] = _dot_nt((q_ref[s, _chunk(j), :] * QK_SCALE).astype(BF16),
                                  k_ref[s, _chunk(j), :].astype(BF16)).astype(BF16)
            pref_b = pref[:, LANES:]
            bcs = (pref[:, :LANES], pref_b[CHUNK - 1:CHUNK, :] - pref_b + lfs[1])
            for d in range(2):
                incl = (r <= c) if d else (r >= c)
                bc = bcs[d]
                ip = ips[d]
                at_col = jnp.transpose(ip - bc)[:CHUNK, :]
                logd = jnp.where(incl, bc[:, :CHUNK] + at_col, -jnp.inf)
                blast = bc[0:1, :] if d else bc[CHUNK - 1:CHUNK, :]
                logw = blast - bc + ip
                bc_s[d, sl, :] = bc
                mx_s[d, sl, :] = jnp.broadcast_to(jnp.max(logd, axis=-1, keepdims=True), (CHUNK, LANES))
                lw_s[d, sl, :] = logw
                ld_s[d, sl, :] = logd
                bl_s[d, a, 0:SUBLANES, :] = jnp.broadcast_to(blast, (SUBLANES, LANES))
                bl_s[d, a, SUBLANES:, :] = jnp.broadcast_to(jnp.max(logw, axis=0, keepdims=True),
                                                            (SUBLANES, LANES))
        return carry

    if nb == 1:
        lax.fori_loop(0, n // group, pre, 0)
    else:
        pre(0, 0)

    chains = [(s, d) for s in range(nb) for d in range(2)]

    def body(i, carry):
        js = [n - 1 - i if d else i for s, d in chains]
        v2s = [jnp.concatenate([v_ref[s, _chunk(j), :], ones], axis=1).astype(BF16)
               for (s, d), j in zip(chains, js)]
        upd, decs, m_news = [], [], []
        for x, ((s, d), j) in enumerate(zip(chains, js)):
            m = carry[2 * x + 1]
            a = s * n + j
            blast = bl_s[d, a, 0:1, :]
            m_new = jnp.maximum(blast + m, bl_s[d, a, SUBLANES:SUBLANES + 1, :])
            wk = (k_ref[s, _chunk(j), :] * jnp.exp(lw_s[d, _chunk(a), :] - m_new)).astype(BF16)
            upd.append(_dot_tn(wk, v2s[x]))
            decs.append(jnp.exp(blast + m - m_new))
            m_news.append(m_new)
        for x, ((s, d), j) in enumerate(zip(chains, js)):
            cn, m = carry[2 * x], carry[2 * x + 1]
            sl = _chunk(s * n + j)
            q = (q_ref[s, _chunk(j), :] * QK_SCALE).astype(BF16)
            m_inter = bc_s[d, sl, :] + m
            m_t = jnp.maximum(m_inter, mx_s[d, sl, :])
            w_inter = jnp.exp(m_inter - m_t)
            sw = qk_s[sl, :].astype(F32) * jnp.exp(ld_s[d, sl, :] - m_t[:, :CHUNK])
            num2 = dup(w_inter) * _dot(q, cn.astype(BF16)) + _dot(sw.astype(BF16), v2s[x])
            h_s[d, sl, :] = num2[:, :LANES] / jnp.maximum(jnp.abs(num2[:, LANES:]), jnp.exp(-m_t))
        out = []
        for x in range(len(chains)):
            out += [dup(decs[x]) * carry[2 * x] + upd[x], m_news[x]]
        return tuple(out)

    init = []
    for s, d in chains:
        if has_init:
            n_b = jnp.broadcast_to(n0_ref[s, d, 0], (HEAD, LANES))
            init += [jnp.concatenate([c0_ref[s, d, 0], n_b], axis=1), m0_ref[s, d, 0]]
        else:
            init += [jnp.zeros((HEAD, 2 * LANES), F32), jnp.zeros((1, LANES), F32)]
    fin = lax.fori_loop(0, n, body, tuple(init), unroll=4)
    if emit_state:
        for x, (s, d) in enumerate(chains):
            cn, m = fin[2 * x], fin[2 * x + 1]
            cfin_ref[s, d, 0] = cn[:, :LANES]
            nfin_ref[s, d, 0] = jnp.transpose(cn[:, LANES:])[0:1, :]
            mfin_ref[s, d, 0] = m

    for s in range(nb):
        def post(j, carry, s=s):
            sl = _chunk(s * n + j)
            so = _chunk(j)
            hh = _rms(h_s[0, sl, :] + h_s[1, sl, :], norm_ref[...])
            o_ref[s, so, :] = (hh * jax.nn.sigmoid(og_ref[s, so, :]) * _silu(z_ref[s, so, :])).astype(BF16)
            return carry

        lax.fori_loop(0, n, post, 0, unroll=2)


def _mlstm_call(proj, par, norm, init, emit_state):
    b, l, _ = proj.shape
    n = l // CHUNK
    nb = _seqs_per_step(b, n)
    base = COL_MLSTM // HEAD

    def col(k):
        return pl.BlockSpec((nb, l, HEAD), lambda i, h: (i, 0, base + k * H_C + h))

    in_specs = [col(0), col(1), col(2), col(3), col(4),
                pl.BlockSpec((nb, l, LANES), lambda i, h: (i, 0, COL_GATES // LANES)),
                pl.BlockSpec((SUBLANES, LANES), lambda i, h: (0, 0)),
                pl.BlockSpec((1, HEAD), lambda i, h: (0, 0))]
    args = [proj, proj, proj, proj, proj, proj, par, norm]
    c_spec = pl.BlockSpec((nb, 2, 1, HEAD, HEAD), lambda i, h: (i, 0, h, 0, 0))
    row_spec = pl.BlockSpec((nb, 2, 1, 1, LANES), lambda i, h: (i, 0, h, 0, 0))
    if init is not None:
        in_specs += [c_spec, pl.BlockSpec((nb, 2, 1, HEAD, 1), lambda i, h: (i, 0, h, 0, 0)), row_spec]
        args += list(init)
    out_shape = [jax.ShapeDtypeStruct((b, l, W_C), BF16)]
    out_specs = [pl.BlockSpec((nb, l, HEAD), lambda i, h: (i, 0, h))]
    if emit_state:
        out_shape += [jax.ShapeDtypeStruct((b, 2, H_C, HEAD, HEAD), F32),
                      jax.ShapeDtypeStruct((b, 2, H_C, 1, LANES), F32),
                      jax.ShapeDtypeStruct((b, 2, H_C, 1, LANES), F32)]
        out_specs += [c_spec, row_spec, row_spec]
    return pl.pallas_call(
        functools.partial(_mlstm_kernel, seq=l, nb=nb, has_init=init is not None, emit_state=emit_state),
        out_shape=out_shape,
        grid=(b // nb, H_C),
        in_specs=in_specs,
        out_specs=out_specs,
        scratch_shapes=[pltpu.VMEM((2, nb * l, LANES), F32),
                        pltpu.VMEM((2, nb * l, LANES), F32),
                        pltpu.VMEM((2, nb * l, LANES), F32),
                        pltpu.VMEM((2, nb * l, CHUNK), F32),
                        pltpu.VMEM((nb * l, CHUNK), BF16),
                        pltpu.VMEM((2, nb * n, 2 * SUBLANES, LANES), F32),
                        pltpu.VMEM((2, nb * l, HEAD), F32)],
        compiler_params=_cparams(("parallel", "parallel")),
        name="mlstm",
    )(*args)


def _rope_swap(w):
    q = AXIS_ROPE // 2
    return jnp.concatenate([-w[..., q:2 * q], w[..., 0:q], -w[..., 3 * q:4 * q], w[..., 2 * q:3 * q]], axis=-1)


def _prep_w_in(w):
    qa_za, ga, ba = w[:, 0:2048], w[:, 2048:2056], w[:, 2056:2064]
    cq, ckv, kr, zb = w[:, 2064:2576], w[:, 2576:2832], w[:, 2832:2896], w[:, 2896:3920]
    mls, ic, fc = w[:, 3920:6480], w[:, 6480:6488], w[:, 6488:6496]
    pad = jnp.zeros((w.shape[0], N_PROJ - COL_GATES - 32), w.dtype)
    return jnp.concatenate([qa_za, zb, cq, mls, ckv, kr, _rope_swap(kr), ga, ba, ic, fc, pad],
                           axis=1).astype(BF16)


def _prep_w_uq(w):
    w = w.reshape(Q_LORA, H_B, HEAD + ROPE_DIM)
    pe = w[..., HEAD:]
    return jnp.concatenate([w[..., :HEAD], pe, _rope_swap(pe)], axis=-1).reshape(Q_LORA, H_B * QK_PAD).astype(BF16)


def _rope_table(seq):
    t = jnp.arange(seq)
    row = (t // GRID_W).astype(F32)
    colp = (t % GRID_W).astype(F32)
    inv = ROPE_THETA ** (-jnp.arange(0, AXIS_ROPE, 2, dtype=F32) / AXIS_ROPE)
    ar = row[:, None] * inv
    ac = colp[:, None] * inv
    cos = jnp.concatenate([jnp.cos(ar), jnp.cos(ar), jnp.cos(ac), jnp.cos(ac)], axis=-1)
    sin = jnp.concatenate([jnp.sin(ar), jnp.sin(ar), jnp.sin(ac), jnp.sin(ac)], axis=-1)
    return jnp.concatenate([cos, sin], axis=-1)


def _gate_params(first, second, lane0_a, lane0_b, extra=None):
    par = jnp.zeros((SUBLANES, LANES), F32)
    if first is not None:
        par = par.at[0, lane0_a:lane0_a + 8].set(first.reshape(-1))
    if second is not None:
        par = par.at[0, lane0_b:lane0_b + 8].set(second.reshape(-1))
    if extra is not None:
        par = par.at[1, lane0_a:lane0_a + 8].set(extra.reshape(-1))
    return par


def kernel(x_prompt, x_sample, cache_mla_ckv, cache_mla_krope, state_gdn, state_mlstm_c, state_mlstm_n,
           state_mlstm_m, c, c_ctx, w_ada, b_ada, g_pre, g_post, w_in, gdn_conv, gdn_a_log, gdn_dt_bias,
           gdn_norm, mla_q_norm, mla_kv_norm, mla_w_uq, mla_w_ukv, mlstm_b_i, mlstm_b_f, mlstm_norm,
           w_out):
    bp, lp, _ = x_prompt.shape
    bs, ls, _ = x_sample.shape

    cond = jnp.concatenate([c, c_ctx[None, :], jnp.zeros((16 - bs - 1, D_MODEL), F32)], axis=0)
    mod = _mod_call(cond, w_ada, b_ada)
    cs_lat = _rope_table(ls)
    cs_ctx = jnp.concatenate([jnp.ones((lp, ROPE_DIM), F32), jnp.zeros((lp, ROPE_DIM), F32)], axis=-1)
    kr_cache = jnp.pad(cache_mla_krope, ((0, 0), (0, 0), (0, 0), (0, LANES - ROPE_DIM)))

    xp = x_prompt.reshape(1, bp * lp, D_MODEL)
    xs = x_sample
    new_ckv, new_kr, new_gdn, new_c, new_n, new_m = [], [], [], [], [], []
    for l in range(DEPTH):
        w_in_l = _prep_w_in(w_in[l])
        wq = _prep_w_uq(mla_w_uq[l])
        wkv = mla_w_ukv[l].astype(BF16)
        w_out_l = w_out[l].astype(BF16)
        mod_l = mod[l].reshape(16, 1, 3 * D_MODEL)
        gdn_par = _gate_params(gdn_dt_bias[l], None, 0, 8, extra=gdn_a_log[l])
        mls_par = _gate_params(mlstm_b_i[l], mlstm_b_f[l], 16, 24)
        g_pre_l = g_pre[l][None, :]
        g_post_l = g_post[l][None, :]
        gdn_norm_l = gdn_norm[l][None, :]
        mls_norm_l = mlstm_norm[l][None, :]
        q_norm_l = mla_q_norm[l][None, :]
        kv_norm_l = mla_kv_norm[l][None, :]

        proj = _inproj_call(xp, g_pre_l, mod_l, bs, w_in_l).reshape(bp, lp, N_PROJ)
        mix_a, s_gdn = _gdn_call(proj, gdn_conv[l], gdn_par, gdn_norm_l, None, True)
        q, k, v, ckvn, kr = _mla_prep_call(proj, cs_ctx, q_norm_l, kv_norm_l, wq, wkv, True)
        mix_b = _attn_call(q, [(k, v)], proj)
        mix_c, cm, nm, mm = _mlstm_call(proj, mls_par, mls_norm_l, None, True)
        xp = _outproj_call(mix_a.reshape(1, bp * lp, W_A), mix_b.reshape(1, bp * lp, W_B),
                           mix_c.reshape(1, bp * lp, W_C), w_out_l, xp, mod_l, bs, g_post_l)
        new_ckv.append(ckvn)
        new_kr.append(kr)
        new_gdn.append(s_gdn)
        new_c.append(cm)
        new_n.append(nm[:, :, :, 0, :])
        new_m.append(mm[:, :, :, 0, 0])

        proj = _inproj_call(xs, g_pre_l, mod_l, 0, w_in_l)
        (mix_a,) = _gdn_call(proj, gdn_conv[l], gdn_par, gdn_norm_l, state_gdn[:, l], False)
        q, k, v = _mla_prep_call(proj, cs_lat, q_norm_l, kv_norm_l, wq, wkv, False)
        k_ctx, v_ctx = _kv_cache_call(cache_mla_ckv[:, l], kr_cache[:, l], wkv)
        mix_b = _attn_call(q, [(k_ctx, v_ctx), (k, v)], proj)
        m0 = jnp.broadcast_to(state_mlstm_m[:, l, :, :, None, None], (bs, 2, H_C, 1, LANES))
        (mix_c,) = _mlstm_call(proj, mls_par, mls_norm_l,
                               (state_mlstm_c[:, l], state_mlstm_n[:, l][..., None], m0), False)
        xs = _outproj_call(mix_a, mix_b, mix_c, w_out_l, xs, mod_l, 0, g_post_l)

    return (xp.reshape(bp, lp, D_MODEL), xs,
            jnp.stack(new_ckv, axis=1), jnp.stack(new_kr, axis=1), jnp.stack(new_gdn, axis=1),
            jnp.stack(new_c, axis=1), jnp.stack(new_n, axis=1), jnp.stack(new_m, axis=1))
```

```python
import functools
import math

import jax
import jax.numpy as jnp
from jax import lax
from jax.experimental import pallas as pl
from jax.experimental.pallas import tpu as pltpu

F32 = jnp.float32
BF16 = jnp.bfloat16

D_MODEL = 2048
DEPTH = 2
GRID_W = 64
EPS = 1e-6
CHUNK = 64
H_A = 4
H_B = 8
H_C = 4
HEAD = 128
Q_LORA = 512
KV_LORA = 256
ROPE_DIM = 64
AXIS_ROPE = ROPE_DIM // 2
ROPE_THETA = 10000.0
SM_SCALE = (HEAD + ROPE_DIM) ** -0.5
Q_SCALE = SM_SCALE * math.log2(math.e)
MIN_ROW_SUM = 2.0 ** -100
KEY_NORM_ROWS = 512
ATTN_AHEAD = 1
Q_SUB = 256
QK_SCALE = HEAD ** -0.5
W_A = H_A * HEAD
W_B = H_B * HEAD
W_C = H_C * HEAD
LANES = 128
SUBLANES = 8
QK_PAD = 256

COL_GDN = 0
COL_ZB = 2048
COL_CQ = 3072
COL_MLSTM = 3584
COL_CKV = 6144
COL_ROPE = 6400
COL_GATES = 6528
N_PROJ = 6656
VMEM_LIMIT = 56 * 1024 * 1024


def _cparams(sem):
    return pltpu.CompilerParams(dimension_semantics=sem, vmem_limit_bytes=VMEM_LIMIT)


def _tile(n, pref):
    t = min(n, pref)
    while n % t:
        t //= 2
    return t


def _dot(a, b):
    return jnp.dot(a, b, preferred_element_type=F32)


def _bdot(a, b):
    return _dot(a.astype(BF16), b.astype(BF16))


def _dot_nt(a, b):
    return lax.dot_general(a, b, (((1,), (1,)), ((), ())), preferred_element_type=F32)


def _dot_tn(a, b):
    return lax.dot_general(a, b, (((0,), (0,)), ((), ())), preferred_element_type=F32)


def _masked_sums(m, x):
    hi = x.astype(BF16)
    lo = (x - hi.astype(F32)).astype(BF16)
    return _dot(m, hi) + _dot(m, lo)


def _pick_lane(x, lane, idx):
    col = jnp.sum(jnp.where(lane == idx, x, 0.0), axis=-1, keepdims=True)
    return jnp.broadcast_to(col, x.shape)


def _rms(x, g):
    return x * lax.rsqrt(jnp.mean(x * x, axis=-1, keepdims=True) + EPS) * g


def _silu(x):
    return x * jax.nn.sigmoid(x)


def _softplus(x):
    return jnp.maximum(x, 0.0) + jnp.log1p(jnp.exp(-jnp.abs(x)))


def _tri_inv_many(mats, r, c):
    eye = (r == c).astype(F32)
    a0 = [jnp.where((r >> 3) == (c >> 3), a, 0.0) for a in mats]
    xs = [eye - a for a in a0]
    a0 = [a.astype(BF16) for a in a0]
    a2 = [_dot(a, a).astype(BF16) for a in a0]
    xs = [x + _dot(x.astype(BF16), a) for x, a in zip(xs, a2)]
    a4 = [_dot(a, a).astype(BF16) for a in a2]
    xs = [x + _dot(x.astype(BF16), a) for x, a in zip(xs, a4)]
    for sh in (4, 5, 6):
        sel = ((r >> sh) == (c >> sh)) != ((r >> (sh - 1)) == (c >> (sh - 1)))
        offs = [jnp.where(sel, a, 0.0).astype(BF16) for a in mats]
        xb = [x.astype(BF16) for x in xs]
        ys = [_dot(x, o).astype(BF16) for x, o in zip(xb, offs)]
        xs = [x - _dot(y, x16) for x, y, x16 in zip(xs, ys, xb)]
    return xs


def _mod_kernel(c_ref, w_ref, b_ref, o_ref):
    o_ref[0] = _bdot(_silu(c_ref[...]), w_ref[0]) + b_ref[0]


def _mod_call(cond, w_ada, b_ada):
    n = w_ada.shape[-1]
    tn = _tile(n, 1536)
    return pl.pallas_call(
        _mod_kernel,
        out_shape=jax.ShapeDtypeStruct((DEPTH, cond.shape[0], n), F32),
        grid=(DEPTH, n // tn),
        in_specs=[pl.BlockSpec(cond.shape, lambda l, j: (0, 0)),
                  pl.BlockSpec((1, D_MODEL, tn), lambda l, j: (l, 0, j)),
                  pl.BlockSpec((1, 1, tn), lambda l, j: (l, 0, j))],
        out_specs=pl.BlockSpec((1, cond.shape[0], tn), lambda l, j: (l, 0, j)),
        compiler_params=_cparams(("parallel", "parallel")),
        name="adaln_mod",
    )(cond, w_ada, b_ada.reshape(DEPTH, 1, n))


def _inproj_kernel(x_ref, g_ref, sh_ref, sc_ref, w_ref, o_ref, h_scr):
    @pl.when(pl.program_id(2) == 0)
    def _():
        h = _rms(x_ref[0], g_ref[...]) * (1.0 + sc_ref[0]) + sh_ref[0]
        h_scr[...] = h.astype(BF16)

    o_ref[0] = _dot(h_scr[...], w_ref[...])


def _inproj_call(x, g_pre, mod, row0, w):
    b, l, _ = x.shape
    tm = _tile(l, 512)
    tn = N_PROJ // 4
    return pl.pallas_call(
        _inproj_kernel,
        out_shape=jax.ShapeDtypeStruct((b, l, N_PROJ), F32),
        grid=(b, l // tm, N_PROJ // tn),
        in_specs=[pl.BlockSpec((1, tm, D_MODEL), lambda i, j, k: (i, j, 0)),
                  pl.BlockSpec((1, D_MODEL), lambda i, j, k: (0, 0)),
                  pl.BlockSpec((1, 1, D_MODEL), lambda i, j, k: (i + row0, 0, 0)),
                  pl.BlockSpec((1, 1, D_MODEL), lambda i, j, k: (i + row0, 0, 1)),
                  pl.BlockSpec((D_MODEL, tn), lambda i, j, k: (0, k))],
        out_specs=pl.BlockSpec((1, tm, tn), lambda i, j, k: (i, j, k)),
        scratch_shapes=[pltpu.VMEM((tm, D_MODEL), BF16)],
        compiler_params=_cparams(("parallel", "parallel", "arbitrary")),
        name="in_proj",
    )(x, g_pre, mod, mod, w)


def _outproj_kernel(a_ref, b_ref, c_ref, w_ref, x_ref, gate_ref, g_ref, o_ref):
    y = (_dot(a_ref[0], w_ref[0:W_A]) + _dot(b_ref[0], w_ref[W_A:W_A + W_B])
         + _dot(c_ref[0], w_ref[W_A + W_B:]))
    o_ref[0] = x_ref[0] + gate_ref[0] * _rms(y, g_ref[...])


def _outproj_call(mix_a, mix_b, mix_c, w, x, mod, row0, g_post):
    b, l, _ = x.shape
    tm = _tile(l, 512)
    return pl.pallas_call(
        _outproj_kernel,
        out_shape=jax.ShapeDtypeStruct(x.shape, F32),
        grid=(b, l // tm),
        in_specs=[pl.BlockSpec((1, tm, W_A), lambda i, j: (i, j, 0)),
                  pl.BlockSpec((1, tm, W_B), lambda i, j: (i, j, 0)),
                  pl.BlockSpec((1, tm, W_C), lambda i, j: (i, j, 0)),
                  pl.BlockSpec((D_MODEL, D_MODEL), lambda i, j: (0, 0)),
                  pl.BlockSpec((1, tm, D_MODEL), lambda i, j: (i, j, 0)),
                  pl.BlockSpec((1, 1, D_MODEL), lambda i, j: (i + row0, 0, 2)),
                  pl.BlockSpec((1, D_MODEL), lambda i, j: (0, 0))],
        out_specs=pl.BlockSpec((1, tm, D_MODEL), lambda i, j: (i, j, 0)),
        compiler_params=_cparams(("parallel", "parallel")),
        name="out_proj",
    )(mix_a, mix_b, mix_c, w, x, mod, g_post)


def _rope_pair(x, cs):
    y = x * cs
    return y + pltpu.roll(y, ROPE_DIM, axis=1)


def _store_kv(kv, kpe, k_out, v_out):
    kpe = kpe.astype(BF16)
    for h in range(H_B):
        k_out[0, :, h * QK_PAD:h * QK_PAD + HEAD] = kv[:, 2 * h * HEAD:(2 * h + 1) * HEAD].astype(BF16)
        k_out[0, :, h * QK_PAD + HEAD:(h + 1) * QK_PAD] = kpe
        v_out[0, :, 2 * h * HEAD:(2 * h + 1) * HEAD] = kv[:, (2 * h + 1) * HEAD:(2 * h + 2) * HEAD].astype(BF16)
        v_out[0, :, (2 * h + 1) * HEAD:(2 * h + 2) * HEAD] = jnp.ones((kv.shape[0], HEAD), BF16)


def _mla_prep_kernel(cq_ref, ckv_ref, kr_ref, cs_ref, qn_ref, kvn_ref, wq_ref, wkv_ref,
                     q_out, k_out, v_out, *cache_out):
    cs = cs_ref[...]
    q = _bdot(_rms(cq_ref[0], qn_ref[...]), wq_ref[...])
    for h in range(H_B):
        q_out[0, :, h * QK_PAD:h * QK_PAD + HEAD] = (
            q[:, h * QK_PAD:h * QK_PAD + HEAD] * Q_SCALE).astype(BF16)
        q_out[0, :, h * QK_PAD + HEAD:(h + 1) * QK_PAD] = (
            _rope_pair(q[:, h * QK_PAD + HEAD:(h + 1) * QK_PAD], cs) * Q_SCALE).astype(BF16)
    ckvn = _rms(ckv_ref[0], kvn_ref[...])
    kr = kr_ref[0]
    lane = lax.broadcasted_iota(jnp.int32, kr.shape, 1)
    kpe = jnp.where(lane < ROPE_DIM, _rope_pair(kr, cs), 0.0)
    _store_kv(_bdot(ckvn, wkv_ref[...]), kpe, k_out, v_out)
    if cache_out:
        cache_out[0][0] = ckvn
        cache_out[1][0] = kr[:, :ROPE_DIM]


def _mla_prep_call(proj, cs, q_norm, kv_norm, wq, wkv, emit_cache):
    b, l, _ = proj.shape
    tm = _tile(l, 512)
    out_shape = [jax.ShapeDtypeStruct((b, l, H_B * QK_PAD), BF16),
                 jax.ShapeDtypeStruct((b, l, H_B * QK_PAD), BF16),
                 jax.ShapeDtypeStruct((b, l, 2 * H_B * HEAD), BF16)]
    out_specs = [pl.BlockSpec((1, tm, H_B * QK_PAD), lambda i, j: (i, j, 0)),
                 pl.BlockSpec((1, tm, H_B * QK_PAD), lambda i, j: (i, j, 0)),
                 pl.BlockSpec((1, tm, 2 * H_B * HEAD), lambda i, j: (i, j, 0))]
    if emit_cache:
        out_shape += [jax.ShapeDtypeStruct((b, l, KV_LORA), F32),
                      jax.ShapeDtypeStruct((b, l, ROPE_DIM), F32)]
        out_specs += [pl.BlockSpec((1, tm, KV_LORA), lambda i, j: (i, j, 0)),
                      pl.BlockSpec((1, tm, ROPE_DIM), lambda i, j: (i, j, 0))]
    return pl.pallas_call(
        _mla_prep_kernel,
        out_shape=out_shape,
        grid=(b, l // tm),
        in_specs=[pl.BlockSpec((1, tm, Q_LORA), lambda i, j: (i, j, COL_CQ // Q_LORA)),
                  pl.BlockSpec((1, tm, KV_LORA), lambda i, j: (i, j, COL_CKV // KV_LORA)),
                  pl.BlockSpec((1, tm, LANES), lambda i, j: (i, j, COL_ROPE // LANES)),
                  pl.BlockSpec((tm, LANES), lambda i, j: (j, 0)),
                  pl.BlockSpec((1, Q_LORA), lambda i, j: (0, 0)),
                  pl.BlockSpec((1, KV_LORA), lambda i, j: (0, 0)),
                  pl.BlockSpec((Q_LORA, H_B * QK_PAD), lambda i, j: (0, 0)),
                  pl.BlockSpec((KV_LORA, 2 * H_B * HEAD), lambda i, j: (0, 0))],
        out_specs=out_specs,
        compiler_params=_cparams(("parallel", "parallel")),
        name="mla_prep",
    )(proj, proj, proj, cs, q_norm, kv_norm, wq, wkv)


def _kv_cache_kernel(ckv_ref, kr_ref, wkv_ref, k_out, v_out):
    _store_kv(_bdot(ckv_ref[0], wkv_ref[...]), kr_ref[0], k_out, v_out)


def _kv_cache_call(ckv, kr_pad, wkv):
    b, p, _ = ckv.shape
    return pl.pallas_call(
        _kv_cache_kernel,
        out_shape=[jax.ShapeDtypeStruct((b, p, H_B * QK_PAD), BF16),
                   jax.ShapeDtypeStruct((b, p, 2 * H_B * HEAD), BF16)],
        grid=(b,),
        in_specs=[pl.BlockSpec((1, p, KV_LORA), lambda i: (i, 0, 0)),
                  pl.BlockSpec((1, p, LANES), lambda i: (i, 0, 0)),
                  pl.BlockSpec((KV_LORA, 2 * H_B * HEAD), lambda i: (0, 0))],
        out_specs=[pl.BlockSpec((1, p, H_B * QK_PAD), lambda i: (i, 0, 0)),
                   pl.BlockSpec((1, p, 2 * H_B * HEAD), lambda i: (i, 0, 0))],
        compiler_params=_cparams(("parallel",)),
        name="mla_cache_kv",
    )(ckv, kr_pad, wkv)


def _attn_kernel(*refs, n_parts):
    q_ref = refs[0]
    kv_refs = refs[1:1 + 2 * n_parts]
    z_ref = refs[1 + 2 * n_parts]
    o_ref = refs[2 + 2 * n_parts]
    kmax_s = refs[3 + 2 * n_parts]
    sub = min(Q_SUB, q_ref.shape[1])
    n_sub = q_ref.shape[1] // sub

    @pl.when(pl.program_id(2) == 0)
    def _():
        best = jnp.zeros((1, 1), F32)
        for p in range(n_parts):
            k_ref = kv_refs[2 * p]
            step = min(k_ref.shape[1], KEY_NORM_ROWS)

            def scan(i, best, k_ref=k_ref, step=step):
                k = k_ref[0, pl.ds(pl.multiple_of(i * step, step), step), :].astype(F32)
                return jnp.maximum(best, jnp.max(jnp.sum(k * k, axis=-1, keepdims=True),
                                                 axis=0, keepdims=True))

            best = lax.fori_loop(0, k_ref.shape[1] // step, scan, best)
        kmax_s[...] = jnp.broadcast_to(jnp.sqrt(best), kmax_s.shape)

    def rows(r):
        return slice(r * sub, (r + 1) * sub)

    def scores(r):
        q = q_ref[0, rows(r), :]
        return [_dot_nt(q, kv_refs[2 * p][0]) for p in range(n_parts)]

    def finish(r, ss, m):
        o2 = functools.reduce(jnp.add, [_dot(jnp.exp2(s - m).astype(BF16), kv_refs[2 * p + 1][0])
                                        for p, s in enumerate(ss)])
        o_ref[0, rows(r), :] = (o2[:, :HEAD] / o2[:, HEAD:] * _silu(z_ref[0, rows(r), :])).astype(BF16)
        return o2[:, HEAD:]

    pending = [scores(r) for r in range(min(ATTN_AHEAD, n_sub))]
    smallest = None
    for r in range(n_sub):
        if r + ATTN_AHEAD < n_sub:
            pending.append(scores(r + ATTN_AHEAD))
        q = q_ref[0, rows(r), :].astype(F32)
        bound = jnp.sqrt(jnp.sum(q * q, axis=-1, keepdims=True)) * kmax_s[0:1, 0:1]
        den = finish(r, pending.pop(0), bound)
        smallest = den if smallest is None else jnp.minimum(smallest, den)

    @pl.when(jnp.min(smallest) < MIN_ROW_SUM)
    def _():
        for r in range(n_sub):
            ss = scores(r)
            finish(r, ss, functools.reduce(jnp.maximum,
                                           [jnp.max(s, axis=-1, keepdims=True) for s in ss]))


def _attn_call(q, kv_parts, proj):
    b, l, _ = q.shape
    tq = _tile(l, 1024)
    in_specs = [pl.BlockSpec((1, tq, QK_PAD), lambda i, h, j: (i, j, h))]
    args = [q]
    for k, v in kv_parts:
        lk = k.shape[1]
        in_specs += [pl.BlockSpec((1, lk, QK_PAD), lambda i, h, j: (i, 0, h)),
                     pl.BlockSpec((1, lk, 2 * HEAD), lambda i, h, j: (i, 0, h))]
        args += [k, v]
    in_specs.append(pl.BlockSpec((1, tq, HEAD), lambda i, h, j: (i, j, COL_ZB // HEAD + h)))
    args.append(proj)
    return pl.pallas_call(
        functools.partial(_attn_kernel, n_parts=len(kv_parts)),
        out_shape=jax.ShapeDtypeStruct((b, l, W_B), BF16),
        grid=(b, H_B, l // tq),
        in_specs=in_specs,
        out_specs=pl.BlockSpec((1, tq, HEAD), lambda i, h, j: (i, j, h)),
        scratch_shapes=[pltpu.VMEM((SUBLANES, LANES), F32)],
        compiler_params=_cparams(("parallel", "parallel", "arbitrary")),
        name="mla_attention",
    )(*args)


def _group(n, most):
    g = most
    while n % g:
        g //= 2
    return g


def _iotas(size):
    return (lax.broadcasted_iota(jnp.int32, (size, size), 0),
            lax.broadcasted_iota(jnp.int32, (size, size), 1))


def _chunk(j):
    return pl.ds(pl.multiple_of(j * CHUNK, CHUNK), CHUNK)


def _gdn_kernel(*refs, seq, nb, has_init, emit_state):
    q_ref, k_ref, v_ref, z_ref, gt_ref, cwq_ref, cwk_ref, cwv_ref, par_ref, norm_ref = refs[:10]
    pos = 10
    s0_ref = None
    if has_init:
        s0_ref = refs[pos]
        pos += 1
    o_ref = refs[pos]
    pos += 1
    sfin_ref = None
    if emit_state:
        sfin_ref = refs[pos]
        pos += 1
    m_s, ku_s, qu_s, egl_s, o_s = refs[pos:pos + 5]

    n = seq // CHUNK
    head = pl.program_id(1)
    group = _group(nb * n, 8)
    r, c = _iotas(2 * CHUNK)

    def where(a):
        return (0, a) if nb == 1 else divmod(a, n)

    lane = lax.broadcasted_iota(jnp.int32, (CHUNK, LANES), 1)
    bias = par_ref[0:1, :]
    neg_a = -jnp.exp(par_ref[1:2, :])

    def conv_silu(ref, w_ref, a):
        s, j = where(a)
        start = pl.multiple_of(j * CHUNK, CHUNK)
        cur = ref[s, pl.ds(start, CHUNK), :]
        prev = ref[s, pl.ds(jnp.maximum(start - SUBLANES, 0), SUBLANES), :]
        prev = jnp.where(j > 0, prev, 0.0)
        nxt = ref[s, pl.ds(jnp.minimum(start + CHUNK, seq - SUBLANES), SUBLANES), :]
        nxt = jnp.where(j < n - 1, nxt, 0.0)
        win = jnp.concatenate([prev, cur, nxt], axis=0)
        w = w_ref[...]
        y = w[0:1, :] * win[6:6 + CHUNK]
        for t in range(1, 5):
            y = y + w[t:t + 1, :] * win[6 + t:6 + t + CHUNK]
        return _silu(y)

    def l2n(x):
        return x * lax.rsqrt(jnp.sum(x * x, axis=-1, keepdims=True) + EPS)

    top = r < CHUNK
    same = (r >> 6) == (c >> 6)
    rd = jnp.where(top, r, -r)
    cd = jnp.where(top, c, -c)
    incl = same & (rd >= cd)
    strict = same & (rd > cd)
    incl_f = incl.astype(F32).astype(BF16)

    def pre(jj, carry):
        pairs = [jj * group + p for p in range(group)]
        qs, ks, vs, gs, betas = [], [], [], [], []
        for a in pairs:
            q = l2n(conv_silu(q_ref, cwq_ref, a)) * QK_SCALE
            k = l2n(conv_silu(k_ref, cwk_ref, a))
            v = conv_silu(v_ref, cwv_ref, a)
            qs.append(jnp.concatenate([q, q], axis=0))
            ks.append(jnp.concatenate([k, k], axis=0))
            vs.append(jnp.concatenate([v, v], axis=0))
            s, j = where(a)
            x = gt_ref[s, _chunk(j), :] + bias
            act = jnp.where(lane < 8, neg_a * _softplus(x), jax.nn.sigmoid(x))
            gs.append(jnp.concatenate([_pick_lane(act, lane, head),
                                       _pick_lane(act, lane, 4 + head)], axis=0))
            betas.append(jnp.concatenate([_pick_lane(act, lane, 8 + head),
                                          _pick_lane(act, lane, 12 + head)], axis=0))
        cols = [_masked_sums(incl_f, g) for g in gs]
        rows = [jnp.transpose(col) for col in cols]
        k16 = [k.astype(BF16) for k in ks]
        kks = [_dot_nt(k, k) for k in k16]
        qks = [_dot_nt(q.astype(BF16), k) for q, k in zip(qs, k16)]
        decays = [jnp.exp(jnp.where(incl, col - row, -jnp.inf)) for col, row in zip(cols, rows)]
        ts = _tri_inv_many([jnp.where(strict, beta * kk * decay, 0.0)
                            for beta, kk, decay in zip(betas, kks, decays)], r, c)
        egs = [jnp.exp(col) for col in cols]
        uws = [_bdot(t, jnp.concatenate([v * beta, k * beta * eg], axis=1)).astype(BF16)
               for t, v, k, beta, eg in zip(ts, vs, ks, betas, egs)]
        glasts = [jnp.where(top, col[CHUNK - 1:CHUNK, :], col[CHUNK:CHUNK + 1, :]) for col in cols]
        kds = [(k * jnp.exp(gl - col)).astype(BF16) for k, gl, col in zip(ks, glasts, cols)]
        qkuw = [_dot((qk * decay).astype(BF16), uw) for qk, decay, uw in zip(qks, decays, uws)]
        for p, a in enumerate(pairs):
            qw = (qs[p] * egs[p] - qkuw[p][:, LANES:]).astype(BF16)
            egl = jnp.exp(glasts[p])
            for d in range(2):
                rs = slice(d * CHUNK, (d + 1) * CHUNK)
                kuw = _dot_tn(kds[p][rs], uws[p][rs])
                m_s[d, a, 0:HEAD, :] = kuw[:, LANES:].astype(BF16)
                m_s[d, a, HEAD:, :] = qw[rs]
                ku_s[d, a] = kuw[:, :LANES]
                qu_s[d, _chunk(a), :] = qkuw[p][rs, :LANES]
                egl_s[d, a] = egl[d * CHUNK:d * CHUNK + SUBLANES]
        return carry

    if nb == 1:
        lax.fori_loop(0, n // group, pre, 0)
    else:
        pre(0, 0)

    chains = [(s, d) for s in range(nb) for d in range(2)]

    def body(i, carry):
        cis = [s * n + (n - 1 - i if d else i) for s, d in chains]
        ms = [_dot(m_s[d, ci], st.astype(BF16)) for (s, d), ci, st in zip(chains, cis, carry)]
        for (s, d), ci, m in zip(chains, cis, ms):
            sl = _chunk(ci)
            o_s[d, sl, :] = m[HEAD:] + qu_s[d, sl, :]
        return tuple(st * egl_s[d, ci][0:1, :] - m[:HEAD] + ku_s[d, ci]
                     for (s, d), ci, st, m in zip(chains, cis, carry, ms))

    if has_init:
        init = tuple(s0_ref[s, d, 0] for s, d in chains)
    else:
        init = tuple(jnp.zeros((HEAD, HEAD), F32) for _ in chains)
    fin = lax.fori_loop(0, n, body, init)
    if emit_state:
        for (s, d), st in zip(chains, fin):
            sfin_ref[s, d, 0] = st

    for s in range(nb):
        def post(j, carry, s=s):
            sl = _chunk(s * n + j)
            o = _rms(o_s[0, sl, :] + o_s[1, sl, :], norm_ref[...])
            o_ref[s, _chunk(j), :] = (o * _silu(z_ref[s, _chunk(j), :])).astype(BF16)
            return carry

        lax.fori_loop(0, n, post, 0, unroll=2)


def _seqs_per_step(b, n):
    nb = 1
    while 2 * nb * n <= 8 and b % (2 * nb) == 0:
        nb *= 2
    return nb


def _gdn_call(proj, conv_w, par, norm, s0, emit_state):
    b, l, _ = proj.shape
    n = l // CHUNK
    nb = _seqs_per_step(b, n)
    base = COL_GDN // HEAD

    def col(k):
        return pl.BlockSpec((nb, l, HEAD), lambda i, h: (i, 0, base + k * H_A + h))

    in_specs = [col(0), col(1), col(2), col(3),
                pl.BlockSpec((nb, l, LANES), lambda i, h: (i, 0, COL_GATES // LANES)),
                pl.BlockSpec((5, HEAD), lambda i, h: (0, h)),
                pl.BlockSpec((5, HEAD), lambda i, h: (0, H_A + h)),
                pl.BlockSpec((5, HEAD), lambda i, h: (0, 2 * H_A + h)),
                pl.BlockSpec((SUBLANES, LANES), lambda i, h: (0, 0)),
                pl.BlockSpec((1, HEAD), lambda i, h: (0, 0))]
    args = [proj, proj, proj, proj, proj, conv_w, conv_w, conv_w, par, norm]
    state_spec = pl.BlockSpec((nb, 2, 1, HEAD, HEAD), lambda i, h: (i, 0, h, 0, 0))
    if s0 is not None:
        in_specs.append(state_spec)
        args.append(s0)
    out_shape = [jax.ShapeDtypeStruct((b, l, W_A), BF16)]
    out_specs = [pl.BlockSpec((nb, l, HEAD), lambda i, h: (i, 0, h))]
    if emit_state:
        out_shape.append(jax.ShapeDtypeStruct((b, 2, H_A, HEAD, HEAD), F32))
        out_specs.append(state_spec)
    return pl.pallas_call(
        functools.partial(_gdn_kernel, seq=l, nb=nb, has_init=s0 is not None, emit_state=emit_state),
        out_shape=out_shape,
        grid=(b // nb, H_A),
        in_specs=in_specs,
        out_specs=out_specs,
        scratch_shapes=[pltpu.VMEM((2, nb * n, HEAD + CHUNK, HEAD), BF16),
                        pltpu.VMEM((2, nb * n, HEAD, HEAD), F32),
                        pltpu.VMEM((2, nb * l, HEAD), F32),
                        pltpu.VMEM((2, nb * n, SUBLANES, LANES), F32),
                        pltpu.VMEM((2, nb * l, HEAD), F32)],
        compiler_params=_cparams(("parallel", "parallel")),
        name="gdn",
    )(*args)


def _mlstm_kernel(*refs, seq, nb, has_init, emit_state):
    q_ref, k_ref, v_ref, og_ref, z_ref, gt_ref, par_ref, norm_ref = refs[:8]
    pos = 8
    if has_init:
        c0_ref, n0_ref, m0_ref = refs[pos:pos + 3]
        pos += 3
    o_ref = refs[pos]
    pos += 1
    if emit_state:
        cfin_ref, nfin_ref, mfin_ref = refs[pos:pos + 3]
        pos += 3
    bc_s, mx_s, lw_s, ld_s, qk_s, bl_s, h_s = refs[pos:pos + 7]

    n = seq // CHUNK
    head = pl.program_id(1)
    group = _group(nb * n, 8)
    r, c = _iotas(CHUNK)

    def where(a):
        return (0, a) if nb == 1 else divmod(a, n)

    lane = lax.broadcasted_iota(jnp.int32, (CHUNK, LANES), 1)
    bias = par_ref[0:1, :]
    low = (r >= c).astype(F32).astype(BF16)
    ones = jnp.ones((CHUNK, LANES), F32)

    def dup(x):
        return jnp.concatenate([x, x], axis=1)

    def pre(jj, carry):
        chunks = [jj * group + p for p in range(group)]
        ipss, lfss = [], []
        for a in chunks:
            s, j = where(a)
            x = gt_ref[s, _chunk(j), :] + bias
            act = jnp.where(lane < 24, x, -_softplus(-x))
            ipss.append([_pick_lane(act, lane, 16 + 4 * d + head) for d in range(2)])
            lfss.append([_pick_lane(act, lane, 24 + 4 * d + head) for d in range(2)])
        prefs = [_masked_sums(low, jnp.concatenate(lfs, axis=1)) for lfs in lfss]
        for a, ips, lfs, pref in zip(chunks, ipss, lfss, prefs):
            s, j = where(a)
            sl = _chunk(a)
            qk_s[sl, :] = _dot_nt((q_ref[s, _chunk(j), :] * QK_SCALE).astype(BF16),
                                  k_ref[s, _chunk(j), :].astype(BF16)).astype(BF16)
            pref_b = pref[:, LANES:]
            bcs = (pref[:, :LANES], pref_b[CHUNK - 1:CHUNK, :] - pref_b + lfs[1])
            for d in range(2):
                incl = (r <= c) if d else (r >= c)
                bc = bcs[d]
                ip = ips[d]
                at_col = jnp.transpose(ip - bc)[:CHUNK, :]
                logd = jnp.where(incl, bc[:, :CHUNK] + at_col, -jnp.inf)
                blast = bc[0:1, :] if d else bc[CHUNK - 1:CHUNK, :]
                logw = blast - bc + ip
                bc_s[d, sl, :] = bc
                mx_s[d, sl, :] = jnp.broadcast_to(jnp.max(logd, axis=-1, keepdims=True), (CHUNK, LANES))
                lw_s[d, sl, :] = logw
                ld_s[d, sl, :] = logd
                bl_s[d, a, 0:SUBLANES, :] = jnp.broadcast_to(blast, (SUBLANES, LANES))
                bl_s[d, a, SUBLANES:, :] = jnp.broadcast_to(jnp.max(logw, axis=0, keepdims=True),
                                                            (SUBLANES, LANES))
        return carry

    if nb == 1:
        lax.fori_loop(0, n // group, pre, 0)
    else:
        pre(0, 0)

    chains = [(s, d) for s in range(nb) for d in range(2)]

    def body(i, carry):
        js = [n - 1 - i if d else i for s, d in chains]
        v2s = [jnp.concatenate([v_ref[s, _chunk(j), :], ones], axis=1).astype(BF16)
               for (s, d), j in zip(chains, js)]
        upd, decs, m_news = [], [], []
        for x, ((s, d), j) in enumerate(zip(chains, js)):
            m = carry[2 * x + 1]
            a = s * n + j
            blast = bl_s[d, a, 0:1, :]
            m_new = jnp.maximum(blast + m, bl_s[d, a, SUBLANES:SUBLANES + 1, :])
            wk = (k_ref[s, _chunk(j), :] * jnp.exp(lw_s[d, _chunk(a), :] - m_new)).astype(BF16)
            upd.append(_dot_tn(wk, v2s[x]))
            decs.append(jnp.exp(blast + m - m_new))
            m_news.append(m_new)
        for x, ((s, d), j) in enumerate(zip(chains, js)):
            cn, m = carry[2 * x], carry[2 * x + 1]
            sl = _chunk(s * n + j)
            q = (q_ref[s, _chunk(j), :] * QK_SCALE).astype(BF16)
            m_inter = bc_s[d, sl, :] + m
            m_t = jnp.maximum(m_inter, mx_s[d, sl, :])
            w_inter = jnp.exp(m_inter - m_t)
            sw = qk_s[sl, :].astype(F32) * jnp.exp(ld_s[d, sl, :] - m_t[:, :CHUNK])
            num2 = dup(w_inter) * _dot(q, cn.astype(BF16)) + _dot(sw.astype(BF16), v2s[x])
            h_s[d, sl, :] = num2[:, :LANES] / jnp.maximum(jnp.abs(num2[:, LANES:]), jnp.exp(-m_t))
        out = []
        for x in range(len(chains)):
            out += [dup(decs[x]) * carry[2 * x] + upd[x], m_news[x]]
        return tuple(out)

    init = []
    for s, d in chains:
        if has_init:
            n_b = jnp.broadcast_to(n0_ref[s, d, 0], (HEAD, LANES))
            init += [jnp.concatenate([c0_ref[s, d, 0], n_b], axis=1), m0_ref[s, d, 0]]
        else:
            init += [jnp.zeros((HEAD, 2 * LANES), F32), jnp.zeros((1, LANES), F32)]
    fin = lax.fori_loop(0, n, body, tuple(init), unroll=4)
    if emit_state:
        for x, (s, d) in enumerate(chains):
            cn, m = fin[2 * x], fin[2 * x + 1]
            cfin_ref[s, d, 0] = cn[:, :LANES]
            nfin_ref[s, d, 0] = jnp.transpose(cn[:, LANES:])[0:1, :]
            mfin_ref[s, d, 0] = m

    for s in range(nb):
        def post(j, carry, s=s):
            sl = _chunk(s * n + j)
            so = _chunk(j)
            hh = _rms(h_s[0, sl, :] + h_s[1, sl, :], norm_ref[...])
            o_ref[s, so, :] = (hh * jax.nn.sigmoid(og_ref[s, so, :]) * _silu(z_ref[s, so, :])).astype(BF16)
            return carry

        lax.fori_loop(0, n, post, 0, unroll=2)


def _mlstm_call(proj, par, norm, init, emit_state):
    b, l, _ = proj.shape
    n = l // CHUNK
    nb = _seqs_per_step(b, n)
    base = COL_MLSTM // HEAD

    def col(k):
        return pl.BlockSpec((nb, l, HEAD), lambda i, h: (i, 0, base + k * H_C + h))

    in_specs = [col(0), col(1), col(2), col(3), col(4),
                pl.BlockSpec((nb, l, LANES), lambda i, h: (i, 0, COL_GATES // LANES)),
                pl.BlockSpec((SUBLANES, LANES), lambda i, h: (0, 0)),
                pl.BlockSpec((1, HEAD), lambda i, h: (0, 0))]
    args = [proj, proj, proj, proj, proj, proj, par, norm]
    c_spec = pl.BlockSpec((nb, 2, 1, HEAD, HEAD), lambda i, h: (i, 0, h, 0, 0))
    row_spec = pl.BlockSpec((nb, 2, 1, 1, LANES), lambda i, h: (i, 0, h, 0, 0))
    if init is not None:
        in_specs += [c_spec, pl.BlockSpec((nb, 2, 1, HEAD, 1), lambda i, h: (i, 0, h, 0, 0)), row_spec]
        args += list(init)
    out_shape = [jax.ShapeDtypeStruct((b, l, W_C), BF16)]
    out_specs = [pl.BlockSpec((nb, l, HEAD), lambda i, h: (i, 0, h))]
    if emit_state:
        out_shape += [jax.ShapeDtypeStruct((b, 2, H_C, HEAD, HEAD), F32),
                      jax.ShapeDtypeStruct((b, 2, H_C, 1, LANES), F32),
                      jax.ShapeDtypeStruct((b, 2, H_C, 1, LANES), F32)]
        out_specs += [c_spec, row_spec, row_spec]
    return pl.pallas_call(
        functools.partial(_mlstm_kernel, seq=l, nb=nb, has_init=init is not None, emit_state=emit_state),
        out_shape=out_shape,
        grid=(b // nb, H_C),
        in_specs=in_specs,
        out_specs=out_specs,
        scratch_shapes=[pltpu.VMEM((2, nb * l, LANES), F32),
                        pltpu.VMEM((2, nb * l, LANES), F32),
                        pltpu.VMEM((2, nb * l, LANES), F32),
                        pltpu.VMEM((2, nb * l, CHUNK), F32),
                        pltpu.VMEM((nb * l, CHUNK), BF16),
                        pltpu.VMEM((2, nb * n, 2 * SUBLANES, LANES), F32),
                        pltpu.VMEM((2, nb * l, HEAD), F32)],
        compiler_params=_cparams(("parallel", "parallel")),
        name="mlstm",
    )(*args)


def _rope_swap(w):
    q = AXIS_ROPE // 2
    return jnp.concatenate([-w[..., q:2 * q], w[..., 0:q], -w[..., 3 * q:4 * q], w[..., 2 * q:3 * q]], axis=-1)


def _prep_w_in(w):
    qa_za, ga, ba = w[:, 0:2048], w[:, 2048:2056], w[:, 2056:2064]
    cq, ckv, kr, zb = w[:, 2064:2576], w[:, 2576:2832], w[:, 2832:2896], w[:, 2896:3920]
    mls, ic, fc = w[:, 3920:6480], w[:, 6480:6488], w[:, 6488:6496]
    pad = jnp.zeros((w.shape[0], N_PROJ - COL_GATES - 32), w.dtype)
    return jnp.concatenate([qa_za, zb, cq, mls, ckv, kr, _rope_swap(kr), ga, ba, ic, fc, pad],
                           axis=1).astype(BF16)


def _prep_w_uq(w):
    w = w.reshape(Q_LORA, H_B, HEAD + ROPE_DIM)
    pe = w[..., HEAD:]
    return jnp.concatenate([w[..., :HEAD], pe, _rope_swap(pe)], axis=-1).reshape(Q_LORA, H_B * QK_PAD).astype(BF16)


def _rope_table(seq):
    t = jnp.arange(seq)
    row = (t // GRID_W).astype(F32)
    colp = (t % GRID_W).astype(F32)
    inv = ROPE_THETA ** (-jnp.arange(0, AXIS_ROPE, 2, dtype=F32) / AXIS_ROPE)
    ar = row[:, None] * inv
    ac = colp[:, None] * inv
    cos = jnp.concatenate([jnp.cos(ar), jnp.cos(ar), jnp.cos(ac), jnp.cos(ac)], axis=-1)
    sin = jnp.concatenate([jnp.sin(ar), jnp.sin(ar), jnp.sin(ac), jnp.sin(ac)], axis=-1)
    return jnp.concatenate([cos, sin], axis=-1)


def _gate_params(first, second, lane0_a, lane0_b, extra=None):
    par = jnp.zeros((SUBLANES, LANES), F32)
    if first is not None:
        par = par.at[0, lane0_a:lane0_a + 8].set(first.reshape(-1))
    if second is not None:
        par = par.at[0, lane0_b:lane0_b + 8].set(second.reshape(-1))
    if extra is not None:
        par = par.at[1, lane0_a:lane0_a + 8].set(extra.reshape(-1))
    return par


def kernel(x_prompt, x_sample, cache_mla_ckv, cache_mla_krope, state_gdn, state_mlstm_c, state_mlstm_n,
           state_mlstm_m, c, c_ctx, w_ada, b_ada, g_pre, g_post, w_in, gdn_conv, gdn_a_log, gdn_dt_bias,
           gdn_norm, mla_q_norm, mla_kv_norm, mla_w_uq, mla_w_ukv, mlstm_b_i, mlstm_b_f, mlstm_norm,
           w_out):
    bp, lp, _ = x_prompt.shape
    bs, ls, _ = x_sample.shape

    cond = jnp.concatenate([c, c_ctx[None, :], jnp.zeros((16 - bs - 1, D_MODEL), F32)], axis=0)
    mod = _mod_call(cond, w_ada, b_ada)
    cs_lat = _rope_table(ls)
    cs_ctx = jnp.concatenate([jnp.ones((lp, ROPE_DIM), F32), jnp.zeros((lp, ROPE_DIM), F32)], axis=-1)
    kr_cache = jnp.pad(cache_mla_krope, ((0, 0), (0, 0), (0, 0), (0, LANES - ROPE_DIM)))

    xp = x_prompt.reshape(1, bp * lp, D_MODEL)
    xs = x_sample
    new_ckv, new_kr, new_gdn, new_c, new_n, new_m = [], [], [], [], [], []
    for l in range(DEPTH):
        w_in_l = _prep_w_in(w_in[l])
        wq = _prep_w_uq(mla_w_uq[l])
        wkv = mla_w_ukv[l].astype(BF16)
        w_out_l = w_out[l].astype(BF16)
        mod_l = mod[l].reshape(16, 1, 3 * D_MODEL)
        gdn_par = _gate_params(gdn_dt_bias[l], None, 0, 8, extra=gdn_a_log[l])
        mls_par = _gate_params(mlstm_b_i[l], mlstm_b_f[l], 16, 24)
        g_pre_l = g_pre[l][None, :]
        g_post_l = g_post[l][None, :]
        gdn_norm_l = gdn_norm[l][None, :]
        mls_norm_l = mlstm_norm[l][None, :]
        q_norm_l = mla_q_norm[l][None, :]
        kv_norm_l = mla_kv_norm[l][None, :]

        proj = _inproj_call(xp, g_pre_l, mod_l, bs, w_in_l).reshape(bp, lp, N_PROJ)
        mix_a, s_gdn = _gdn_call(proj, gdn_conv[l], gdn_par, gdn_norm_l, None, True)
        q, k, v, ckvn, kr = _mla_prep_call(proj, cs_ctx, q_norm_l, kv_norm_l, wq, wkv, True)
        mix_b = _attn_call(q, [(k, v)], proj)
        mix_c, cm, nm, mm = _mlstm_call(proj, mls_par, mls_norm_l, None, True)
        xp = _outproj_call(mix_a.reshape(1, bp * lp, W_A), mix_b.reshape(1, bp * lp, W_B),
                           mix_c.reshape(1, bp * lp, W_C), w_out_l, xp, mod_l, bs, g_post_l)
        new_ckv.append(ckvn)
        new_kr.append(kr)
        new_gdn.append(s_gdn)
        new_c.append(cm)
        new_n.append(nm[:, :, :, 0, :])
        new_m.append(mm[:, :, :, 0, 0])

        proj = _inproj_call(xs, g_pre_l, mod_l, 0, w_in_l)
        (mix_a,) = _gdn_call(proj, gdn_conv[l], gdn_par, gdn_norm_l, state_gdn[:, l], False)
        q, k, v = _mla_prep_call(proj, cs_lat, q_norm_l, kv_norm_l, wq, wkv, False)
        k_ctx, v_ctx = _kv_cache_call(cache_mla_ckv[:, l], kr_cache[:, l], wkv)
        mix_b = _attn_call(q, [(k_ctx, v_ctx), (k, v)], proj)
        m0 = jnp.broadcast_to(state_mlstm_m[:, l, :, :, None, None], (bs, 2, H_C, 1, LANES))
        (mix_c,) = _mlstm_call(proj, mls_par, mls_norm_l,
                               (state_mlstm_c[:, l], state_mlstm_n[:, l][..., None], m0), False)
        xs = _outproj_call(mix_a, mix_b, mix_c, w_out_l, xs, mod_l, 0, g_post_l)

    return (xp.reshape(bp, lp, D_MODEL), xs,
            jnp.stack(new_ckv, axis=1), jnp.stack(new_kr, axis=1), jnp.stack(new_gdn, axis=1),
            jnp.stack(new_c, axis=1), jnp.stack(new_n, axis=1), jnp.stack(new_m, axis=1))
```

```python
import functools
import math

import jax
import jax.numpy as jnp
from jax import lax
from jax.experimental import pallas as pl
from jax.experimental.pallas import tpu as pltpu

F32 = jnp.float32
BF16 = jnp.bfloat16

D_MODEL = 2048
DEPTH = 2
GRID_W = 64
EPS = 1e-6
CHUNK = 64
H_A = 4
H_B = 8
H_C = 4
HEAD = 128
Q_LORA = 512
KV_LORA = 256
ROPE_DIM = 64
AXIS_ROPE = ROPE_DIM // 2
ROPE_THETA = 10000.0
SM_SCALE = (HEAD + ROPE_DIM) ** -0.5
Q_SCALE = SM_SCALE * math.log2(math.e)
MIN_ROW_SUM = 2.0 ** -100
KEY_NORM_ROWS = 512
ATTN_AHEAD = 1
Q_SUB = 256
QK_SCALE = HEAD ** -0.5
W_A = H_A * HEAD
W_B = H_B * HEAD
W_C = H_C * HEAD
LANES = 128
SUBLANES = 8
QK_PAD = 256

COL_GDN = 0
COL_ZB = 2048
COL_CQ = 3072
COL_MLSTM = 3584
COL_CKV = 6144
COL_ROPE = 6400
COL_GATES = 6528
N_PROJ = 6656
VMEM_LIMIT = 56 * 1024 * 1024


def _cparams(sem):
    return pltpu.CompilerParams(dimension_semantics=sem, vmem_limit_bytes=VMEM_LIMIT)


def _tile(n, pref):
    t = min(n, pref)
    while n % t:
        t //= 2
    return t


def _dot(a, b):
    return jnp.dot(a, b, preferred_element_type=F32)


def _bdot(a, b):
    return _dot(a.astype(BF16), b.astype(BF16))


def _dot_nt(a, b):
    return lax.dot_general(a, b, (((1,), (1,)), ((), ())), preferred_element_type=F32)


def _dot_tn(a, b):
    return lax.dot_general(a, b, (((0,), (0,)), ((), ())), preferred_element_type=F32)


def _masked_sums(m, x):
    hi = x.astype(BF16)
    lo = (x - hi.astype(F32)).astype(BF16)
    return _dot(m, hi) + _dot(m, lo)


def _pick_lane(x, lane, idx):
    col = jnp.sum(jnp.where(lane == idx, x, 0.0), axis=-1, keepdims=True)
    return jnp.broadcast_to(col, x.shape)


def _rms(x, g):
    return x * lax.rsqrt(jnp.mean(x * x, axis=-1, keepdims=True) + EPS) * g


def _silu(x):
    return x * jax.nn.sigmoid(x)


def _softplus(x):
    return jnp.maximum(x, 0.0) + jnp.log1p(jnp.exp(-jnp.abs(x)))


def _tri_inv_many(mats, r, c):
    eye = (r == c).astype(F32)
    a0 = [jnp.where((r >> 3) == (c >> 3), a, 0.0) for a in mats]
    xs = [eye - a for a in a0]
    a0 = [a.astype(BF16) for a in a0]
    a2 = [_dot(a, a).astype(BF16) for a in a0]
    xs = [x + _dot(x.astype(BF16), a) for x, a in zip(xs, a2)]
    a4 = [_dot(a, a).astype(BF16) for a in a2]
    xs = [x + _dot(x.astype(BF16), a) for x, a in zip(xs, a4)]
    for sh in (4, 5, 6):
        sel = ((r >> sh) == (c >> sh)) != ((r >> (sh - 1)) == (c >> (sh - 1)))
        offs = [jnp.where(sel, a, 0.0).astype(BF16) for a in mats]
        xb = [x.astype(BF16) for x in xs]
        ys = [_dot(x, o).astype(BF16) for x, o in zip(xb, offs)]
        xs = [x - _dot(y, x16) for x, y, x16 in zip(xs, ys, xb)]
    return xs


def _mod_kernel(c_ref, w_ref, b_ref, o_ref):
    o_ref[0] = _bdot(_silu(c_ref[...]), w_ref[0]) + b_ref[0]


def _mod_call(cond, w_ada, b_ada):
    n = w_ada.shape[-1]
    tn = _tile(n, 1536)
    return pl.pallas_call(
        _mod_kernel,
        out_shape=jax.ShapeDtypeStruct((DEPTH, cond.shape[0], n), F32),
        grid=(DEPTH, n // tn),
        in_specs=[pl.BlockSpec(cond.shape, lambda l, j: (0, 0)),
                  pl.BlockSpec((1, D_MODEL, tn), lambda l, j: (l, 0, j)),
                  pl.BlockSpec((1, 1, tn), lambda l, j: (l, 0, j))],
        out_specs=pl.BlockSpec((1, cond.shape[0], tn), lambda l, j: (l, 0, j)),
        compiler_params=_cparams(("parallel", "parallel")),
        name="adaln_mod",
    )(cond, w_ada, b_ada.reshape(DEPTH, 1, n))


def _inproj_kernel(x_ref, g_ref, sh_ref, sc_ref, w_ref, o_ref, h_scr):
    @pl.when(pl.program_id(2) == 0)
    def _():
        h = _rms(x_ref[0], g_ref[...]) * (1.0 + sc_ref[0]) + sh_ref[0]
        h_scr[...] = h.astype(BF16)

    o_ref[0] = _dot(h_scr[...], w_ref[...])


def _inproj_call(x, g_pre, mod, row0, w):
    b, l, _ = x.shape
    tm = _tile(l, 512)
    tn = N_PROJ // 4
    return pl.pallas_call(
        _inproj_kernel,
        out_shape=jax.ShapeDtypeStruct((b, l, N_PROJ), F32),
        grid=(b, l // tm, N_PROJ // tn),
        in_specs=[pl.BlockSpec((1, tm, D_MODEL), lambda i, j, k: (i, j, 0)),
                  pl.BlockSpec((1, D_MODEL), lambda i, j, k: (0, 0)),
                  pl.BlockSpec((1, 1, D_MODEL), lambda i, j, k: (i + row0, 0, 0)),
                  pl.BlockSpec((1, 1, D_MODEL), lambda i, j, k: (i + row0, 0, 1)),
                  pl.BlockSpec((D_MODEL, tn), lambda i, j, k: (0, k))],
        out_specs=pl.BlockSpec((1, tm, tn), lambda i, j, k: (i, j, k)),
        scratch_shapes=[pltpu.VMEM((tm, D_MODEL), BF16)],
        compiler_params=_cparams(("parallel", "parallel", "arbitrary")),
        name="in_proj",
    )(x, g_pre, mod, mod, w)


def _outproj_kernel(a_ref, b_ref, c_ref, w_ref, x_ref, gate_ref, g_ref, o_ref):
    y = (_dot(a_ref[0], w_ref[0:W_A]) + _dot(b_ref[0], w_ref[W_A:W_A + W_B])
         + _dot(c_ref[0], w_ref[W_A + W_B:]))
    o_ref[0] = x_ref[0] + gate_ref[0] * _rms(y, g_ref[...])


def _outproj_call(mix_a, mix_b, mix_c, w, x, mod, row0, g_post):
    b, l, _ = x.shape
    tm = _tile(l, 512)
    return pl.pallas_call(
        _outproj_kernel,
        out_shape=jax.ShapeDtypeStruct(x.shape, F32),
        grid=(b, l // tm),
        in_specs=[pl.BlockSpec((1, tm, W_A), lambda i, j: (i, j, 0)),
                  pl.BlockSpec((1, tm, W_B), lambda i, j: (i, j, 0)),
                  pl.BlockSpec((1, tm, W_C), lambda i, j: (i, j, 0)),
                  pl.BlockSpec((D_MODEL, D_MODEL), lambda i, j: (0, 0)),
                  pl.BlockSpec((1, tm, D_MODEL), lambda i, j: (i, j, 0)),
                  pl.BlockSpec((1, 1, D_MODEL), lambda i, j: (i + row0, 0, 2)),
                  pl.BlockSpec((1, D_MODEL), lambda i, j: (0, 0))],
        out_specs=pl.BlockSpec((1, tm, D_MODEL), lambda i, j: (i, j, 0)),
        compiler_params=_cparams(("parallel", "parallel")),
        name="out_proj",
    )(mix_a, mix_b, mix_c, w, x, mod, g_post)


def _rope_pair(x, cs):
    y = x * cs
    return y + pltpu.roll(y, ROPE_DIM, axis=1)


def _store_kv(kv, kpe, k_out, v_out):
    kpe = kpe.astype(BF16)
    for h in range(H_B):
        k_out[0, :, h * QK_PAD:h * QK_PAD + HEAD] = kv[:, 2 * h * HEAD:(2 * h + 1) * HEAD].astype(BF16)
        k_out[0, :, h * QK_PAD + HEAD:(h + 1) * QK_PAD] = kpe
        v_out[0, :, 2 * h * HEAD:(2 * h + 1) * HEAD] = kv[:, (2 * h + 1) * HEAD:(2 * h + 2) * HEAD].astype(BF16)
        v_out[0, :, (2 * h + 1) * HEAD:(2 * h + 2) * HEAD] = jnp.ones((kv.shape[0], HEAD), BF16)


def _mla_prep_kernel(cq_ref, ckv_ref, kr_ref, cs_ref, qn_ref, kvn_ref, wq_ref, wkv_ref,
                     q_out, k_out, v_out, *cache_out):
    cs = cs_ref[...]
    q = _bdot(_rms(cq_ref[0], qn_ref[...]), wq_ref[...])
    for h in range(H_B):
        q_out[0, :, h * QK_PAD:h * QK_PAD + HEAD] = (
            q[:, h * QK_PAD:h * QK_PAD + HEAD] * Q_SCALE).astype(BF16)
        q_out[0, :, h * QK_PAD + HEAD:(h + 1) * QK_PAD] = (
            _rope_pair(q[:, h * QK_PAD + HEAD:(h + 1) * QK_PAD], cs) * Q_SCALE).astype(BF16)
    ckvn = _rms(ckv_ref[0], kvn_ref[...])
    kr = kr_ref[0]
    lane = lax.broadcasted_iota(jnp.int32, kr.shape, 1)
    kpe = jnp.where(lane < ROPE_DIM, _rope_pair(kr, cs), 0.0)
    _store_kv(_bdot(ckvn, wkv_ref[...]), kpe, k_out, v_out)
    if cache_out:
        cache_out[0][0] = ckvn
        cache_out[1][0] = kr[:, :ROPE_DIM]


def _mla_prep_call(proj, cs, q_norm, kv_norm, wq, wkv, emit_cache):
    b, l, _ = proj.shape
    tm = _tile(l, 512)
    out_shape = [jax.ShapeDtypeStruct((b, l, H_B * QK_PAD), BF16),
                 jax.ShapeDtypeStruct((b, l, H_B * QK_PAD), BF16),
                 jax.ShapeDtypeStruct((b, l, 2 * H_B * HEAD), BF16)]
    out_specs = [pl.BlockSpec((1, tm, H_B * QK_PAD), lambda i, j: (i, j, 0)),
                 pl.BlockSpec((1, tm, H_B * QK_PAD), lambda i, j: (i, j, 0)),
                 pl.BlockSpec((1, tm, 2 * H_B * HEAD), lambda i, j: (i, j, 0))]
    if emit_cache:
        out_shape += [jax.ShapeDtypeStruct((b, l, KV_LORA), F32),
                      jax.ShapeDtypeStruct((b, l, ROPE_DIM), F32)]
        out_specs += [pl.BlockSpec((1, tm, KV_LORA), lambda i, j: (i, j, 0)),
                      pl.BlockSpec((1, tm, ROPE_DIM), lambda i, j: (i, j, 0))]
    return pl.pallas_call(
        _mla_prep_kernel,
        out_shape=out_shape,
        grid=(b, l // tm),
        in_specs=[pl.BlockSpec((1, tm, Q_LORA), lambda i, j: (i, j, COL_CQ // Q_LORA)),
                  pl.BlockSpec((1, tm, KV_LORA), lambda i, j: (i, j, COL_CKV // KV_LORA)),
                  pl.BlockSpec((1, tm, LANES), lambda i, j: (i, j, COL_ROPE // LANES)),
                  pl.BlockSpec((tm, LANES), lambda i, j: (j, 0)),
                  pl.BlockSpec((1, Q_LORA), lambda i, j: (0, 0)),
                  pl.BlockSpec((1, KV_LORA), lambda i, j: (0, 0)),
                  pl.BlockSpec((Q_LORA, H_B * QK_PAD), lambda i, j: (0, 0)),
                  pl.BlockSpec((KV_LORA, 2 * H_B * HEAD), lambda i, j: (0, 0))],
        out_specs=out_specs,
        compiler_params=_cparams(("parallel", "parallel")),
        name="mla_prep",
    )(proj, proj, proj, cs, q_norm, kv_norm, wq, wkv)


def _kv_cache_kernel(ckv_ref, kr_ref, wkv_ref, k_out, v_out):
    _store_kv(_bdot(ckv_ref[0], wkv_ref[...]), kr_ref[0], k_out, v_out)


def _kv_cache_call(ckv, kr_pad, wkv):
    b, p, _ = ckv.shape
    return pl.pallas_call(
        _kv_cache_kernel,
        out_shape=[jax.ShapeDtypeStruct((b, p, H_B * QK_PAD), BF16),
                   jax.ShapeDtypeStruct((b, p, 2 * H_B * HEAD), BF16)],
        grid=(b,),
        in_specs=[pl.BlockSpec((1, p, KV_LORA), lambda i: (i, 0, 0)),
                  pl.BlockSpec((1, p, LANES), lambda i: (i, 0, 0)),
                  pl.BlockSpec((KV_LORA, 2 * H_B * HEAD), lambda i: (0, 0))],
        out_specs=[pl.BlockSpec((1, p, H_B * QK_PAD), lambda i: (i, 0, 0)),
                   pl.BlockSpec((1, p, 2 * H_B * HEAD), lambda i: (i, 0, 0))],
        compiler_params=_cparams(("parallel",)),
        name="mla_cache_kv",
    )(ckv, kr_pad, wkv)


def _attn_kernel(*refs, n_parts):
    q_ref = refs[0]
    kv_refs = refs[1:1 + 2 * n_parts]
    z_ref = refs[1 + 2 * n_parts]
    o_ref = refs[2 + 2 * n_parts]
    kmax_s = refs[3 + 2 * n_parts]
    sub = min(Q_SUB, q_ref.shape[1])
    n_sub = q_ref.shape[1] // sub

    @pl.when(pl.program_id(2) == 0)
    def _():
        best = jnp.zeros((1, 1), F32)
        for p in range(n_parts):
            k_ref = kv_refs[2 * p]
            step = min(k_ref.shape[1], KEY_NORM_ROWS)

            def scan(i, best, k_ref=k_ref, step=step):
                k = k_ref[0, pl.ds(pl.multiple_of(i * step, step), step), :].astype(F32)
                return jnp.maximum(best, jnp.max(jnp.sum(k * k, axis=-1, keepdims=True),
                                                 axis=0, keepdims=True))

            best = lax.fori_loop(0, k_ref.shape[1] // step, scan, best)
        kmax_s[...] = jnp.broadcast_to(jnp.sqrt(best), kmax_s.shape)

    def rows(r):
        return slice(r * sub, (r + 1) * sub)

    def scores(r):
        q = q_ref[0, rows(r), :]
        return [_dot_nt(q, kv_refs[2 * p][0]) for p in range(n_parts)]

    def finish(r, ss, m):
        o2 = functools.reduce(jnp.add, [_dot(jnp.exp2(s - m).astype(BF16), kv_refs[2 * p + 1][0])
                                        for p, s in enumerate(ss)])
        o_ref[0, rows(r), :] = (o2[:, :HEAD] / o2[:, HEAD:] * _silu(z_ref[0, rows(r), :])).astype(BF16)
        return o2[:, HEAD:]

    pending = [scores(r) for r in range(min(ATTN_AHEAD, n_sub))]
    smallest = None
    for r in range(n_sub):
        if r + ATTN_AHEAD < n_sub:
            pending.append(scores(r + ATTN_AHEAD))
        q = q_ref[0, rows(r), :].astype(F32)
        bound = jnp.sqrt(jnp.sum(q * q, axis=-1, keepdims=True)) * kmax_s[0:1, 0:1]
        den = finish(r, pending.pop(0), bound)
        smallest = den if smallest is None else jnp.minimum(smallest, den)

    @pl.when(jnp.min(smallest) < MIN_ROW_SUM)
    def _():
        for r in range(n_sub):
            ss = scores(r)
            finish(r, ss, functools.reduce(jnp.maximum,
                                           [jnp.max(s, axis=-1, keepdims=True) for s in ss]))


def _attn_call(q, kv_parts, proj):
    b, l, _ = q.shape
    tq = _tile(l, 1024)
    in_specs = [pl.BlockSpec((1, tq, QK_PAD), lambda i, h, j: (i, j, h))]
    args = [q]
    for k, v in kv_parts:
        lk = k.shape[1]
        in_specs += [pl.BlockSpec((1, lk, QK_PAD), lambda i, h, j: (i, 0, h)),
                     pl.BlockSpec((1, lk, 2 * HEAD), lambda i, h, j: (i, 0, h))]
        args += [k, v]
    in_specs.append(pl.BlockSpec((1, tq, HEAD), lambda i, h, j: (i, j, COL_ZB // HEAD + h)))
    args.append(proj)
    return pl.pallas_call(
        functools.partial(_attn_kernel, n_parts=len(kv_parts)),
        out_shape=jax.ShapeDtypeStruct((b, l, W_B), BF16),
        grid=(b, H_B, l // tq),
        in_specs=in_specs,
        out_specs=pl.BlockSpec((1, tq, HEAD), lambda i, h, j: (i, j, h)),
        scratch_shapes=[pltpu.VMEM((SUBLANES, LANES), F32)],
        compiler_params=_cparams(("parallel", "parallel", "arbitrary")),
        name="mla_attention",
    )(*args)


def _group(n, most):
    g = most
    while n % g:
        g //= 2
    return g


def _iotas(size):
    return (lax.broadcasted_iota(jnp.int32, (size, size), 0),
            lax.broadcasted_iota(jnp.int32, (size, size), 1))


def _chunk(j):
    return pl.ds(pl.multiple_of(j * CHUNK, CHUNK), CHUNK)


def _gdn_kernel(*refs, seq, nb, has_init, emit_state):
    q_ref, k_ref, v_ref, z_ref, gt_ref, cwq_ref, cwk_ref, cwv_ref, par_ref, norm_ref = refs[:10]
    pos = 10
    s0_ref = None
    if has_init:
        s0_ref = refs[pos]
        pos += 1
    o_ref = refs[pos]
    pos += 1
    sfin_ref = None
    if emit_state:
        sfin_ref = refs[pos]
        pos += 1
    m_s, ku_s, qu_s, egl_s, pad_s = refs[pos:pos + 5]

    n = seq // CHUNK
    head = pl.program_id(1)
    group = _group(nb * n, 8)
    r, c = _iotas(2 * CHUNK)

    def where(a):
        return (0, a) if nb == 1 else divmod(a, n)

    lane = lax.broadcasted_iota(jnp.int32, (CHUNK, LANES), 1)
    bias = par_ref[0:1, :]
    neg_a = -jnp.exp(par_ref[1:2, :])

    for which, ref in enumerate((q_ref, k_ref, v_ref)):
        for s in range(nb):
            pad_s[which, s, 0:SUBLANES, :] = jnp.zeros((SUBLANES, HEAD), F32)
            pad_s[which, s, SUBLANES + seq:, :] = jnp.zeros((SUBLANES, HEAD), F32)

            def fill(j, carry, which=which, ref=ref, s=s):
                pad_s[which, s, pl.ds(pl.multiple_of(j * CHUNK + SUBLANES, SUBLANES), CHUNK), :] = (
                    ref[s, _chunk(j), :])
                return carry

            lax.fori_loop(0, n, fill, 0, unroll=4 if n % 4 == 0 else 1)

    def conv_silu(which, w_ref, a):
        s, j = where(a)
        w = w_ref[...]
        y = None
        for t in range(5):
            tap = w[t:t + 1, :] * pad_s[which, s, pl.ds(j * CHUNK + SUBLANES - 2 + t, CHUNK), :]
            y = tap if y is None else y + tap
        return _silu(y)

    def l2n(x):
        return x * lax.rsqrt(jnp.sum(x * x, axis=-1, keepdims=True) + EPS)

    top = r < CHUNK
    same = (r >> 6) == (c >> 6)
    rd = jnp.where(top, r, -r)
    cd = jnp.where(top, c, -c)
    incl = same & (rd >= cd)
    strict = same & (rd > cd)
    incl_f = incl.astype(F32).astype(BF16)

    def pre(jj, carry):
        pairs = [jj * group + p for p in range(group)]
        qs, ks, vs, gs, betas = [], [], [], [], []
        for a in pairs:
            q = l2n(conv_silu(0, cwq_ref, a)) * QK_SCALE
            k = l2n(conv_silu(1, cwk_ref, a))
            v = conv_silu(2, cwv_ref, a)
            qs.append(jnp.concatenate([q, q], axis=0))
            ks.append(jnp.concatenate([k, k], axis=0))
            vs.append(jnp.concatenate([v, v], axis=0))
            s, j = where(a)
            x = gt_ref[s, _chunk(j), :] + bias
            act = jnp.where(lane < 8, neg_a * _softplus(x), jax.nn.sigmoid(x))
            gs.append(jnp.concatenate([_pick_lane(act, lane, head),
                                       _pick_lane(act, lane, 4 + head)], axis=0))
            betas.append(jnp.concatenate([_pick_lane(act, lane, 8 + head),
                                          _pick_lane(act, lane, 12 + head)], axis=0))
        cols = [_masked_sums(incl_f, g) for g in gs]
        rows = [jnp.transpose(col) for col in cols]
        k16 = [k.astype(BF16) for k in ks]
        kks = [_dot_nt(k, k) for k in k16]
        qks = [_dot_nt(q.astype(BF16), k) for q, k in zip(qs, k16)]
        decays = [jnp.exp(jnp.where(incl, col - row, -jnp.inf)) for col, row in zip(cols, rows)]
        ts = _tri_inv_many([jnp.where(strict, beta * kk * decay, 0.0)
                            for beta, kk, decay in zip(betas, kks, decays)], r, c)
        egs = [jnp.exp(col) for col in cols]
        uws = [_bdot(t, jnp.concatenate([v * beta, k * beta * eg], axis=1)).astype(BF16)
               for t, v, k, beta, eg in zip(ts, vs, ks, betas, egs)]
        glasts = [jnp.where(top, col[CHUNK - 1:CHUNK, :], col[CHUNK:CHUNK + 1, :]) for col in cols]
        kds = [(k * jnp.exp(gl - col)).astype(BF16) for k, gl, col in zip(ks, glasts, cols)]
        qkuw = [_dot((qk * decay).astype(BF16), uw) for qk, decay, uw in zip(qks, decays, uws)]
        for p, a in enumerate(pairs):
            qw = (qs[p] * egs[p] - qkuw[p][:, LANES:]).astype(BF16)
            egl = jnp.exp(glasts[p])
            for d in range(2):
                rs = slice(d * CHUNK, (d + 1) * CHUNK)
                kuw = _dot_tn(kds[p][rs], uws[p][rs])
                m_s[d, a, 0:HEAD, :] = kuw[:, LANES:].astype(BF16)
                m_s[d, a, HEAD:, :] = qw[rs]
                ku_s[d, a] = kuw[:, :LANES]
                qu_s[d, _chunk(a), :] = qkuw[p][rs, :LANES]
                egl_s[d, a] = egl[d * CHUNK:d * CHUNK + SUBLANES]
        return carry

    if nb == 1:
        lax.fori_loop(0, n // group, pre, 0)
    else:
        pre(0, 0)

    chains = [(s, d) for s in range(nb) for d in range(2)]

    def body(i, carry):
        cis = [s * n + (n - 1 - i if d else i) for s, d in chains]
        ms = [_dot(m_s[d, ci], st.astype(BF16)) for (s, d), ci, st in zip(chains, cis, carry)]
        for (s, d), ci, m in zip(chains, cis, ms):
            sl = _chunk(ci)
            qu_s[d, sl, :] = m[HEAD:] + qu_s[d, sl, :]
        return tuple(st * egl_s[d, ci][0:1, :] - m[:HEAD] + ku_s[d, ci]
                     for (s, d), ci, st, m in zip(chains, cis, carry, ms))

    if has_init:
        init = tuple(s0_ref[s, d, 0] for s, d in chains)
    else:
        init = tuple(jnp.zeros((HEAD, HEAD), F32) for _ in chains)
    fin = lax.fori_loop(0, n, body, init)
    if emit_state:
        for (s, d), st in zip(chains, fin):
            sfin_ref[s, d, 0] = st

    for s in range(nb):
        def post(j, carry, s=s):
            sl = _chunk(s * n + j)
            o = _rms(qu_s[0, sl, :] + qu_s[1, sl, :], norm_ref[...])
            o_ref[s, _chunk(j), :] = (o * _silu(z_ref[s, _chunk(j), :])).astype(BF16)
            return carry

        lax.fori_loop(0, n, post, 0, unroll=2)


def _seqs_per_step(b, n):
    nb = 1
    while 2 * nb * n <= 8 and b % (2 * nb) == 0:
        nb *= 2
    return nb


def _gdn_call(proj, conv_w, par, norm, s0, emit_state):
    b, l, _ = proj.shape
    n = l // CHUNK
    nb = _seqs_per_step(b, n)
    base = COL_GDN // HEAD

    def col(k):
        return pl.BlockSpec((nb, l, HEAD), lambda i, h: (i, 0, base + k * H_A + h))

    in_specs = [col(0), col(1), col(2), col(3),
                pl.BlockSpec((nb, l, LANES), lambda i, h: (i, 0, COL_GATES // LANES)),
                pl.BlockSpec((5, HEAD), lambda i, h: (0, h)),
                pl.BlockSpec((5, HEAD), lambda i, h: (0, H_A + h)),
                pl.BlockSpec((5, HEAD), lambda i, h: (0, 2 * H_A + h)),
                pl.BlockSpec((SUBLANES, LANES), lambda i, h: (0, 0)),
                pl.BlockSpec((1, HEAD), lambda i, h: (0, 0))]
    args = [proj, proj, proj, proj, proj, conv_w, conv_w, conv_w, par, norm]
    state_spec = pl.BlockSpec((nb, 2, 1, HEAD, HEAD), lambda i, h: (i, 0, h, 0, 0))
    if s0 is not None:
        in_specs.append(state_spec)
        args.append(s0)
    out_shape = [jax.ShapeDtypeStruct((b, l, W_A), BF16)]
    out_specs = [pl.BlockSpec((nb, l, HEAD), lambda i, h: (i, 0, h))]
    if emit_state:
        out_shape.append(jax.ShapeDtypeStruct((b, 2, H_A, HEAD, HEAD), F32))
        out_specs.append(state_spec)
    return pl.pallas_call(
        functools.partial(_gdn_kernel, seq=l, nb=nb, has_init=s0 is not None, emit_state=emit_state),
        out_shape=out_shape,
        grid=(b // nb, H_A),
        in_specs=in_specs,
        out_specs=out_specs,
        scratch_shapes=[pltpu.VMEM((2, nb * n, HEAD + CHUNK, HEAD), BF16),
                        pltpu.VMEM((2, nb * n, HEAD, HEAD), F32),
                        pltpu.VMEM((2, nb * l, HEAD), F32),
                        pltpu.VMEM((2, nb * n, SUBLANES, LANES), F32),
                        pltpu.VMEM((3, nb, l + 2 * SUBLANES, HEAD), F32)],
        compiler_params=_cparams(("parallel", "parallel")),
        name="gdn",
    )(*args)


def _mlstm_kernel(*refs, seq, nb, has_init, emit_state):
    q_ref, k_ref, v_ref, og_ref, z_ref, gt_ref, par_ref, norm_ref = refs[:8]
    pos = 8
    if has_init:
        c0_ref, n0_ref, m0_ref = refs[pos:pos + 3]
        pos += 3
    o_ref = refs[pos]
    pos += 1
    if emit_state:
        cfin_ref, nfin_ref, mfin_ref = refs[pos:pos + 3]
        pos += 3
    bc_s, mx_s, lw_s, ld_s, qk_s, bl_s, h_s = refs[pos:pos + 7]

    n = seq // CHUNK
    head = pl.program_id(1)
    group = _group(nb * n, 8)
    r, c = _iotas(CHUNK)

    def where(a):
        return (0, a) if nb == 1 else divmod(a, n)

    lane = lax.broadcasted_iota(jnp.int32, (CHUNK, LANES), 1)
    bias = par_ref[0:1, :]
    low = (r >= c).astype(F32).astype(BF16)
    ones = jnp.ones((CHUNK, LANES), F32)

    def dup(x):
        return jnp.concatenate([x, x], axis=1)

    def pre(jj, carry):
        chunks = [jj * group + p for p in range(group)]
        ipss, lfss = [], []
        for a in chunks:
            s, j = where(a)
            x = gt_ref[s, _chunk(j), :] + bias
            act = jnp.where(lane < 24, x, -_softplus(-x))
            ipss.append([_pick_lane(act, lane, 16 + 4 * d + head) for d in range(2)])
            lfss.append([_pick_lane(act, lane, 24 + 4 * d + head) for d in range(2)])
        prefs = [_masked_sums(low, jnp.concatenate(lfs, axis=1)) for lfs in lfss]
        for a, ips, lfs, pref in zip(chunks, ipss, lfss, prefs):
            s, j = where(a)
            sl = _chunk(a)
            qk_s[sl, :] = _dot_nt((q_ref[s, _chunk(j), :] * QK_SCALE).astype(BF16),
                                  k_ref[s, _chunk(j), :].astype(BF16)).astype(BF16)
            pref_b = pref[:, LANES:]
            bcs = (pref[:, :LANES], pref_b[CHUNK - 1:CHUNK, :] - pref_b + lfs[1])
            for d in range(2):
                incl = (r <= c) if d else (r >= c)
                bc = bcs[d]
                ip = ips[d]
                at_col = jnp.transpose(ip - bc)[:CHUNK, :]
                logd = jnp.where(incl, bc[:, :CHUNK] + at_col, -jnp.inf)
                blast = bc[0:1, :] if d else bc[CHUNK - 1:CHUNK, :]
                logw = blast - bc + ip
                bc_s[d, sl, :] = bc
                mx_s[d, sl, :] = jnp.broadcast_to(jnp.max(logd, axis=-1, keepdims=True), (CHUNK, LANES))
                lw_s[d, sl, :] = logw
                ld_s[d, sl, :] = logd
                bl_s[d, a, 0:SUBLANES, :] = jnp.broadcast_to(blast, (SUBLANES, LANES))
                bl_s[d, a, SUBLANES:, :] = jnp.broadcast_to(jnp.max(logw, axis=0, keepdims=True),
                                                            (SUBLANES, LANES))
        return carry

    if nb == 1:
        lax.fori_loop(0, n // group, pre, 0)
    else:
        pre(0, 0)

    chains = [(s, d) for s in range(nb) for d in range(2)]

    def body(i, carry):
        js = [n - 1 - i if d else i for s, d in chains]
        v2s = [jnp.concatenate([v_ref[s, _chunk(j), :], ones], axis=1).astype(BF16)
               for (s, d), j in zip(chains, js)]
        upd, decs, m_news = [], [], []
        for x, ((s, d), j) in enumerate(zip(chains, js)):
            m = carry[2 * x + 1]
            a = s * n + j
            blast = bl_s[d, a, 0:1, :]
            m_new = jnp.maximum(blast + m, bl_s[d, a, SUBLANES:SUBLANES + 1, :])
            wk = (k_ref[s, _chunk(j), :] * jnp.exp(lw_s[d, _chunk(a), :] - m_new)).astype(BF16)
            upd.append(_dot_tn(wk, v2s[x]))
            decs.append(jnp.exp(blast + m - m_new))
            m_news.append(m_new)
        for x, ((s, d), j) in enumerate(zip(chains, js)):
            cn, m = carry[2 * x], carry[2 * x + 1]
            sl = _chunk(s * n + j)
            q = (q_ref[s, _chunk(j), :] * QK_SCALE).astype(BF16)
            m_inter = bc_s[d, sl, :] + m
            m_t = jnp.maximum(m_inter, mx_s[d, sl, :])
            w_inter = jnp.exp(m_inter - m_t)
            sw = qk_s[sl, :].astype(F32) * jnp.exp(ld_s[d, sl, :] - m_t[:, :CHUNK])
            num2 = dup(w_inter) * _dot(q, cn.astype(BF16)) + _dot(sw.astype(BF16), v2s[x])
            h_s[d, sl, :] = num2[:, :LANES] / jnp.maximum(jnp.abs(num2[:, LANES:]), jnp.exp(-m_t))
        out = []
        for x in range(len(chains)):
            out += [dup(decs[x]) * carry[2 * x] + upd[x], m_news[x]]
        return tuple(out)

    init = []
    for s, d in chains:
        if has_init:
            n_b = jnp.broadcast_to(n0_ref[s, d, 0], (HEAD, LANES))
            init += [jnp.concatenate([c0_ref[s, d, 0], n_b], axis=1), m0_ref[s, d, 0]]
        else:
            init += [jnp.zeros((HEAD, 2 * LANES), F32), jnp.zeros((1, LANES), F32)]
    fin = lax.fori_loop(0, n, body, tuple(init), unroll=4)
    if emit_state:
        for x, (s, d) in enumerate(chains):
            cn, m = fin[2 * x], fin[2 * x + 1]
            cfin_ref[s, d, 0] = cn[:, :LANES]
            nfin_ref[s, d, 0] = jnp.transpose(cn[:, LANES:])[0:1, :]
            mfin_ref[s, d, 0] = m

    for s in range(nb):
        def post(j, carry, s=s):
            sl = _chunk(s * n + j)
            so = _chunk(j)
            hh = _rms(h_s[0, sl, :] + h_s[1, sl, :], norm_ref[...])
            o_ref[s, so, :] = (hh * jax.nn.sigmoid(og_ref[s, so, :]) * _silu(z_ref[s, so, :])).astype(BF16)
            return carry

        lax.fori_loop(0, n, post, 0, unroll=2)


def _mlstm_call(proj, par, norm, init, emit_state):
    b, l, _ = proj.shape
    n = l // CHUNK
    nb = _seqs_per_step(b, n)
    base = COL_MLSTM // HEAD

    def col(k):
        return pl.BlockSpec((nb, l, HEAD), lambda i, h: (i, 0, base + k * H_C + h))

    in_specs = [col(0), col(1), col(2), col(3), col(4),
                pl.BlockSpec((nb, l, LANES), lambda i, h: (i, 0, COL_GATES // LANES)),
                pl.BlockSpec((SUBLANES, LANES), lambda i, h: (0, 0)),
                pl.BlockSpec((1, HEAD), lambda i, h: (0, 0))]
    args = [proj, proj, proj, proj, proj, proj, par, norm]
    c_spec = pl.BlockSpec((nb, 2, 1, HEAD, HEAD), lambda i, h: (i, 0, h, 0, 0))
    row_spec = pl.BlockSpec((nb, 2, 1, 1, LANES), lambda i, h: (i, 0, h, 0, 0))
    if init is not None:
        in_specs += [c_spec, pl.BlockSpec((nb, 2, 1, HEAD, 1), lambda i, h: (i, 0, h, 0, 0)), row_spec]
        args += list(init)
    out_shape = [jax.ShapeDtypeStruct((b, l, W_C), BF16)]
    out_specs = [pl.BlockSpec((nb, l, HEAD), lambda i, h: (i, 0, h))]
    if emit_state:
        out_shape += [jax.ShapeDtypeStruct((b, 2, H_C, HEAD, HEAD), F32),
                      jax.ShapeDtypeStruct((b, 2, H_C, 1, LANES), F32),
                      jax.ShapeDtypeStruct((b, 2, H_C, 1, LANES), F32)]
        out_specs += [c_spec, row_spec, row_spec]
    return pl.pallas_call(
        functools.partial(_mlstm_kernel, seq=l, nb=nb, has_init=init is not None, emit_state=emit_state),
        out_shape=out_shape,
        grid=(b // nb, H_C),
        in_specs=in_specs,
        out_specs=out_specs,
        scratch_shapes=[pltpu.VMEM((2, nb * l, LANES), F32),
                        pltpu.VMEM((2, nb * l, LANES), F32),
                        pltpu.VMEM((2, nb * l, LANES), F32),
                        pltpu.VMEM((2, nb * l, CHUNK), F32),
                        pltpu.VMEM((nb * l, CHUNK), BF16),
                        pltpu.VMEM((2, nb * n, 2 * SUBLANES, LANES), F32),
                        pltpu.VMEM((2, nb * l, HEAD), F32)],
        compiler_params=_cparams(("parallel", "parallel")),
        name="mlstm",
    )(*args)


def _rope_swap(w):
    q = AXIS_ROPE // 2
    return jnp.concatenate([-w[..., q:2 * q], w[..., 0:q], -w[..., 3 * q:4 * q], w[..., 2 * q:3 * q]], axis=-1)


def _prep_w_in(w):
    qa_za, ga, ba = w[:, 0:2048], w[:, 2048:2056], w[:, 2056:2064]
    cq, ckv, kr, zb = w[:, 2064:2576], w[:, 2576:2832], w[:, 2832:2896], w[:, 2896:3920]
    mls, ic, fc = w[:, 3920:6480], w[:, 6480:6488], w[:, 6488:6496]
    pad = jnp.zeros((w.shape[0], N_PROJ - COL_GATES - 32), w.dtype)
    return jnp.concatenate([qa_za, zb, cq, mls, ckv, kr, _rope_swap(kr), ga, ba, ic, fc, pad],
                           axis=1).astype(BF16)


def _prep_w_uq(w):
    w = w.reshape(Q_LORA, H_B, HEAD + ROPE_DIM)
    pe = w[..., HEAD:]
    return jnp.concatenate([w[..., :HEAD], pe, _rope_swap(pe)], axis=-1).reshape(Q_LORA, H_B * QK_PAD).astype(BF16)


def _rope_table(seq):
    t = jnp.arange(seq)
    row = (t // GRID_W).astype(F32)
    colp = (t % GRID_W).astype(F32)
    inv = ROPE_THETA ** (-jnp.arange(0, AXIS_ROPE, 2, dtype=F32) / AXIS_ROPE)
    ar = row[:, None] * inv
    ac = colp[:, None] * inv
    cos = jnp.concatenate([jnp.cos(ar), jnp.cos(ar), jnp.cos(ac), jnp.cos(ac)], axis=-1)
    sin = jnp.concatenate([jnp.sin(ar), jnp.sin(ar), jnp.sin(ac), jnp.sin(ac)], axis=-1)
    return jnp.concatenate([cos, sin], axis=-1)


def _gate_params(first, second, lane0_a, lane0_b, extra=None):
    par = jnp.zeros((SUBLANES, LANES), F32)
    if first is not None:
        par = par.at[0, lane0_a:lane0_a + 8].set(first.reshape(-1))
    if second is not None:
        par = par.at[0, lane0_b:lane0_b + 8].set(second.reshape(-1))
    if extra is not None:
        par = par.at[1, lane0_a:lane0_a + 8].set(extra.reshape(-1))
    return par


def kernel(x_prompt, x_sample, cache_mla_ckv, cache_mla_krope, state_gdn, state_mlstm_c, state_mlstm_n,
           state_mlstm_m, c, c_ctx, w_ada, b_ada, g_pre, g_post, w_in, gdn_conv, gdn_a_log, gdn_dt_bias,
           gdn_norm, mla_q_norm, mla_kv_norm, mla_w_uq, mla_w_ukv, mlstm_b_i, mlstm_b_f, mlstm_norm,
           w_out):
    bp, lp, _ = x_prompt.shape
    bs, ls, _ = x_sample.shape

    cond = jnp.concatenate([c, c_ctx[None, :], jnp.zeros((16 - bs - 1, D_MODEL), F32)], axis=0)
    mod = _mod_call(cond, w_ada, b_ada)
    cs_lat = _rope_table(ls)
    cs_ctx = jnp.concatenate([jnp.ones((lp, ROPE_DIM), F32), jnp.zeros((lp, ROPE_DIM), F32)], axis=-1)
    kr_cache = jnp.pad(cache_mla_krope, ((0, 0), (0, 0), (0, 0), (0, LANES - ROPE_DIM)))

    xp = x_prompt.reshape(1, bp * lp, D_MODEL)
    xs = x_sample
    new_ckv, new_kr, new_gdn, new_c, new_n, new_m = [], [], [], [], [], []
    for l in range(DEPTH):
        w_in_l = _prep_w_in(w_in[l])
        wq = _prep_w_uq(mla_w_uq[l])
        wkv = mla_w_ukv[l].astype(BF16)
        w_out_l = w_out[l].astype(BF16)
        mod_l = mod[l].reshape(16, 1, 3 * D_MODEL)
        gdn_par = _gate_params(gdn_dt_bias[l], None, 0, 8, extra=gdn_a_log[l])
        mls_par = _gate_params(mlstm_b_i[l], mlstm_b_f[l], 16, 24)
        g_pre_l = g_pre[l][None, :]
        g_post_l = g_post[l][None, :]
        gdn_norm_l = gdn_norm[l][None, :]
        mls_norm_l = mlstm_norm[l][None, :]
        q_norm_l = mla_q_norm[l][None, :]
        kv_norm_l = mla_kv_norm[l][None, :]

        proj = _inproj_call(xp, g_pre_l, mod_l, bs, w_in_l).reshape(bp, lp, N_PROJ)
        mix_a, s_gdn = _gdn_call(proj, gdn_conv[l], gdn_par, gdn_norm_l, None, True)
        q, k, v, ckvn, kr = _mla_prep_call(proj, cs_ctx, q_norm_l, kv_norm_l, wq, wkv, True)
        mix_b = _attn_call(q, [(k, v)], proj)
        mix_c, cm, nm, mm = _mlstm_call(proj, mls_par, mls_norm_l, None, True)
        xp = _outproj_call(mix_a.reshape(1, bp * lp, W_A), mix_b.reshape(1, bp * lp, W_B),
                           mix_c.reshape(1, bp * lp, W_C), w_out_l, xp, mod_l, bs, g_post_l)
        new_ckv.append(ckvn)
        new_kr.append(kr)
        new_gdn.append(s_gdn)
        new_c.append(cm)
        new_n.append(nm[:, :, :, 0, :])
        new_m.append(mm[:, :, :, 0, 0])

        proj = _inproj_call(xs, g_pre_l, mod_l, 0, w_in_l)
        (mix_a,) = _gdn_call(proj, gdn_conv[l], gdn_par, gdn_norm_l, state_gdn[:, l], False)
        q, k, v = _mla_prep_call(proj, cs_lat, q_norm_l, kv_norm_l, wq, wkv, False)
        k_ctx, v_ctx = _kv_cache_call(cache_mla_ckv[:, l], kr_cache[:, l], wkv)
        mix_b = _attn_call(q, [(k_ctx, v_ctx), (k, v)], proj)
        m0 = jnp.broadcast_to(state_mlstm_m[:, l, :, :, None, None], (bs, 2, H_C, 1, LANES))
        (mix_c,) = _mlstm_call(proj, mls_par, mls_norm_l,
                               (state_mlstm_c[:, l], state_mlstm_n[:, l][..., None], m0), False)
        xs = _outproj_call(mix_a, mix_b, mix_c, w_out_l, xs, mod_l, 0, g_post_l)

    return (xp.reshape(bp, lp, D_MODEL), xs,
            jnp.stack(new_ckv, axis=1), jnp.stack(new_kr, axis=1), jnp.stack(new_gdn, axis=1),
            jnp.stack(new_c, axis=1), jnp.stack(new_n, axis=1), jnp.stack(new_m, axis=1))
```

```python
import functools
import math

import jax
import jax.numpy as jnp
from jax import lax
from jax.experimental import pallas as pl
from jax.experimental.pallas import tpu as pltpu

F32 = jnp.float32
BF16 = jnp.bfloat16

D_MODEL = 2048
DEPTH = 2
GRID_W = 64
EPS = 1e-6
CHUNK = 64
H_A = 4
H_B = 8
H_C = 4
HEAD = 128
Q_LORA = 512
KV_LORA = 256
ROPE_DIM = 64
AXIS_ROPE = ROPE_DIM // 2
ROPE_THETA = 10000.0
SM_SCALE = (HEAD + ROPE_DIM) ** -0.5
Q_SCALE = SM_SCALE * math.log2(math.e)
MIN_ROW_SUM = 2.0 ** -100
KEY_NORM_ROWS = 512
ATTN_AHEAD = 1
Q_SUB = 256
QK_SCALE = HEAD ** -0.5
W_A = H_A * HEAD
W_B = H_B * HEAD
W_C = H_C * HEAD
LANES = 128
SUBLANES = 8
QK_PAD = 256

COL_GDN = 0
COL_ZB = 2048
COL_CQ = 3072
COL_MLSTM = 3584
COL_CKV = 6144
COL_ROPE = 6400
COL_GATES = 6528
N_PROJ = 6656
VMEM_LIMIT = 56 * 1024 * 1024


def _cparams(sem):
    return pltpu.CompilerParams(dimension_semantics=sem, vmem_limit_bytes=VMEM_LIMIT)


def _tile(n, pref):
    t = min(n, pref)
    while n % t:
        t //= 2
    return t


def _dot(a, b):
    return jnp.dot(a, b, preferred_element_type=F32)


def _bdot(a, b):
    return _dot(a.astype(BF16), b.astype(BF16))


def _dot_nt(a, b):
    return lax.dot_general(a, b, (((1,), (1,)), ((), ())), preferred_element_type=F32)


def _dot_tn(a, b):
    return lax.dot_general(a, b, (((0,), (0,)), ((), ())), preferred_element_type=F32)


def _masked_sums(m, x):
    hi = x.astype(BF16)
    lo = (x - hi.astype(F32)).astype(BF16)
    return _dot(m, hi) + _dot(m, lo)


def _pick_lane(x, lane, idx):
    col = jnp.sum(jnp.where(lane == idx, x, 0.0), axis=-1, keepdims=True)
    return jnp.broadcast_to(col, x.shape)


def _rms(x, g):
    return x * lax.rsqrt(jnp.mean(x * x, axis=-1, keepdims=True) + EPS) * g


def _silu(x):
    return x * jax.nn.sigmoid(x)


def _softplus(x):
    return jnp.maximum(x, 0.0) + jnp.log1p(jnp.exp(-jnp.abs(x)))


def _tri_inv_many(mats, r, c):
    eye = (r == c).astype(F32)
    a0 = [jnp.where((r >> 3) == (c >> 3), a, 0.0) for a in mats]
    xs = [eye - a for a in a0]
    a0 = [a.astype(BF16) for a in a0]
    a2 = [_dot(a, a).astype(BF16) for a in a0]
    xs = [x + _dot(x.astype(BF16), a) for x, a in zip(xs, a2)]
    a4 = [_dot(a, a).astype(BF16) for a in a2]
    xs = [x + _dot(x.astype(BF16), a) for x, a in zip(xs, a4)]
    for sh in (4, 5, 6):
        sel = ((r >> sh) == (c >> sh)) != ((r >> (sh - 1)) == (c >> (sh - 1)))
        offs = [jnp.where(sel, a, 0.0).astype(BF16) for a in mats]
        xb = [x.astype(BF16) for x in xs]
        ys = [_dot(x, o).astype(BF16) for x, o in zip(xb, offs)]
        xs = [x - _dot(y, x16) for x, y, x16 in zip(xs, ys, xb)]
    return xs


def _mod_kernel(c_ref, w_ref, b_ref, o_ref):
    o_ref[0] = _bdot(_silu(c_ref[...]), w_ref[0]) + b_ref[0]


def _mod_call(cond, w_ada, b_ada):
    n = w_ada.shape[-1]
    tn = _tile(n, 1536)
    return pl.pallas_call(
        _mod_kernel,
        out_shape=jax.ShapeDtypeStruct((DEPTH, cond.shape[0], n), F32),
        grid=(DEPTH, n // tn),
        in_specs=[pl.BlockSpec(cond.shape, lambda l, j: (0, 0)),
                  pl.BlockSpec((1, D_MODEL, tn), lambda l, j: (l, 0, j)),
                  pl.BlockSpec((1, 1, tn), lambda l, j: (l, 0, j))],
        out_specs=pl.BlockSpec((1, cond.shape[0], tn), lambda l, j: (l, 0, j)),
        compiler_params=_cparams(("parallel", "parallel")),
        name="adaln_mod",
    )(cond, w_ada, b_ada.reshape(DEPTH, 1, n))


def _inproj_kernel(x_ref, g_ref, sh_ref, sc_ref, w_ref, o_ref, h_scr):
    @pl.when(pl.program_id(2) == 0)
    def _():
        h = _rms(x_ref[0], g_ref[...]) * (1.0 + sc_ref[0]) + sh_ref[0]
        h_scr[...] = h.astype(BF16)

    o_ref[0] = _dot(h_scr[...], w_ref[...])


def _inproj_call(x, g_pre, mod, row0, w):
    b, l, _ = x.shape
    tm = _tile(l, 512)
    tn = N_PROJ // 4
    return pl.pallas_call(
        _inproj_kernel,
        out_shape=jax.ShapeDtypeStruct((b, l, N_PROJ), F32),
        grid=(b, l // tm, N_PROJ // tn),
        in_specs=[pl.BlockSpec((1, tm, D_MODEL), lambda i, j, k: (i, j, 0)),
                  pl.BlockSpec((1, D_MODEL), lambda i, j, k: (0, 0)),
                  pl.BlockSpec((1, 1, D_MODEL), lambda i, j, k: (i + row0, 0, 0)),
                  pl.BlockSpec((1, 1, D_MODEL), lambda i, j, k: (i + row0, 0, 1)),
                  pl.BlockSpec((D_MODEL, tn), lambda i, j, k: (0, k))],
        out_specs=pl.BlockSpec((1, tm, tn), lambda i, j, k: (i, j, k)),
        scratch_shapes=[pltpu.VMEM((tm, D_MODEL), BF16)],
        compiler_params=_cparams(("parallel", "parallel", "arbitrary")),
        name="in_proj",
    )(x, g_pre, mod, mod, w)


def _outproj_kernel(a_ref, b_ref, c_ref, w_ref, x_ref, gate_ref, g_ref, o_ref):
    y = (_dot(a_ref[0], w_ref[0:W_A]) + _dot(b_ref[0], w_ref[W_A:W_A + W_B])
         + _dot(c_ref[0], w_ref[W_A + W_B:]))
    o_ref[0] = x_ref[0] + gate_ref[0] * _rms(y, g_ref[...])


def _outproj_call(mix_a, mix_b, mix_c, w, x, mod, row0, g_post):
    b, l, _ = x.shape
    tm = _tile(l, 512)
    return pl.pallas_call(
        _outproj_kernel,
        out_shape=jax.ShapeDtypeStruct(x.shape, F32),
        grid=(b, l // tm),
        in_specs=[pl.BlockSpec((1, tm, W_A), lambda i, j: (i, j, 0)),
                  pl.BlockSpec((1, tm, W_B), lambda i, j: (i, j, 0)),
                  pl.BlockSpec((1, tm, W_C), lambda i, j: (i, j, 0)),
                  pl.BlockSpec((D_MODEL, D_MODEL), lambda i, j: (0, 0)),
                  pl.BlockSpec((1, tm, D_MODEL), lambda i, j: (i, j, 0)),
                  pl.BlockSpec((1, 1, D_MODEL), lambda i, j: (i + row0, 0, 2)),
                  pl.BlockSpec((1, D_MODEL), lambda i, j: (0, 0))],
        out_specs=pl.BlockSpec((1, tm, D_MODEL), lambda i, j: (i, j, 0)),
        compiler_params=_cparams(("parallel", "parallel")),
        name="out_proj",
    )(mix_a, mix_b, mix_c, w, x, mod, g_post)


def _rope_pair(x, cs):
    y = x * cs
    return y + pltpu.roll(y, ROPE_DIM, axis=1)


def _store_kv(kv, kpe, k_out, v_out):
    kpe = kpe.astype(BF16)
    for h in range(H_B):
        k_out[0, :, h * QK_PAD:h * QK_PAD + HEAD] = kv[:, 2 * h * HEAD:(2 * h + 1) * HEAD].astype(BF16)
        k_out[0, :, h * QK_PAD + HEAD:(h + 1) * QK_PAD] = kpe
        v_out[0, :, 2 * h * HEAD:(2 * h + 1) * HEAD] = kv[:, (2 * h + 1) * HEAD:(2 * h + 2) * HEAD].astype(BF16)
        v_out[0, :, (2 * h + 1) * HEAD:(2 * h + 2) * HEAD] = jnp.ones((kv.shape[0], HEAD), BF16)


def _mla_prep_kernel(cq_ref, ckv_ref, kr_ref, cs_ref, qn_ref, kvn_ref, wq_ref, wkv_ref,
                     q_out, k_out, v_out, *cache_out):
    cs = cs_ref[...]
    q = _bdot(_rms(cq_ref[0], qn_ref[...]), wq_ref[...])
    for h in range(H_B):
        q_out[0, :, h * QK_PAD:h * QK_PAD + HEAD] = (
            q[:, h * QK_PAD:h * QK_PAD + HEAD] * Q_SCALE).astype(BF16)
        q_out[0, :, h * QK_PAD + HEAD:(h + 1) * QK_PAD] = (
            _rope_pair(q[:, h * QK_PAD + HEAD:(h + 1) * QK_PAD], cs) * Q_SCALE).astype(BF16)
    ckvn = _rms(ckv_ref[0], kvn_ref[...])
    kr = kr_ref[0]
    lane = lax.broadcasted_iota(jnp.int32, kr.shape, 1)
    kpe = jnp.where(lane < ROPE_DIM, _rope_pair(kr, cs), 0.0)
    _store_kv(_bdot(ckvn, wkv_ref[...]), kpe, k_out, v_out)
    if cache_out:
        cache_out[0][0] = ckvn
        cache_out[1][0] = kr[:, :ROPE_DIM]


def _mla_prep_call(proj, cs, q_norm, kv_norm, wq, wkv, emit_cache):
    b, l, _ = proj.shape
    tm = _tile(l, 512)
    out_shape = [jax.ShapeDtypeStruct((b, l, H_B * QK_PAD), BF16),
                 jax.ShapeDtypeStruct((b, l, H_B * QK_PAD), BF16),
                 jax.ShapeDtypeStruct((b, l, 2 * H_B * HEAD), BF16)]
    out_specs = [pl.BlockSpec((1, tm, H_B * QK_PAD), lambda i, j: (i, j, 0)),
                 pl.BlockSpec((1, tm, H_B * QK_PAD), lambda i, j: (i, j, 0)),
                 pl.BlockSpec((1, tm, 2 * H_B * HEAD), lambda i, j: (i, j, 0))]
    if emit_cache:
        out_shape += [jax.ShapeDtypeStruct((b, l, KV_LORA), F32),
                      jax.ShapeDtypeStruct((b, l, ROPE_DIM), F32)]
        out_specs += [pl.BlockSpec((1, tm, KV_LORA), lambda i, j: (i, j, 0)),
                      pl.BlockSpec((1, tm, ROPE_DIM), lambda i, j: (i, j, 0))]
    return pl.pallas_call(
        _mla_prep_kernel,
        out_shape=out_shape,
        grid=(b, l // tm),
        in_specs=[pl.BlockSpec((1, tm, Q_LORA), lambda i, j: (i, j, COL_CQ // Q_LORA)),
                  pl.BlockSpec((1, tm, KV_LORA), lambda i, j: (i, j, COL_CKV // KV_LORA)),
                  pl.BlockSpec((1, tm, LANES), lambda i, j: (i, j, COL_ROPE // LANES)),
                  pl.BlockSpec((tm, LANES), lambda i, j: (j, 0)),
                  pl.BlockSpec((1, Q_LORA), lambda i, j: (0, 0)),
                  pl.BlockSpec((1, KV_LORA), lambda i, j: (0, 0)),
                  pl.BlockSpec((Q_LORA, H_B * QK_PAD), lambda i, j: (0, 0)),
                  pl.BlockSpec((KV_LORA, 2 * H_B * HEAD), lambda i, j: (0, 0))],
        out_specs=out_specs,
        compiler_params=_cparams(("parallel", "parallel")),
        name="mla_prep",
    )(proj, proj, proj, cs, q_norm, kv_norm, wq, wkv)


def _kv_cache_kernel(ckv_ref, kr_ref, wkv_ref, k_out, v_out):
    _store_kv(_bdot(ckv_ref[0], wkv_ref[...]), kr_ref[0], k_out, v_out)


def _kv_cache_call(ckv, kr_pad, wkv):
    b, p, _ = ckv.shape
    return pl.pallas_call(
        _kv_cache_kernel,
        out_shape=[jax.ShapeDtypeStruct((b, p, H_B * QK_PAD), BF16),
                   jax.ShapeDtypeStruct((b, p, 2 * H_B * HEAD), BF16)],
        grid=(b,),
        in_specs=[pl.BlockSpec((1, p, KV_LORA), lambda i: (i, 0, 0)),
                  pl.BlockSpec((1, p, LANES), lambda i: (i, 0, 0)),
                  pl.BlockSpec((KV_LORA, 2 * H_B * HEAD), lambda i: (0, 0))],
        out_specs=[pl.BlockSpec((1, p, H_B * QK_PAD), lambda i: (i, 0, 0)),
                   pl.BlockSpec((1, p, 2 * H_B * HEAD), lambda i: (i, 0, 0))],
        compiler_params=_cparams(("parallel",)),
        name="mla_cache_kv",
    )(ckv, kr_pad, wkv)


def _attn_kernel(*refs, n_parts, heads):
    q_ref = refs[0]
    kv_refs = refs[1:1 + 2 * n_parts]
    z_ref = refs[1 + 2 * n_parts]
    o_ref = refs[2 + 2 * n_parts]
    kmax_s = refs[3 + 2 * n_parts]
    sub = min(Q_SUB, q_ref.shape[1])
    n_sub = q_ref.shape[1] // sub

    def qk_cols(hh):
        return slice(hh * QK_PAD, (hh + 1) * QK_PAD)

    def out_cols(hh):
        return slice(hh * HEAD, (hh + 1) * HEAD)

    @pl.when(pl.program_id(2) == 0)
    def _():
        for hh in range(heads):
            best = jnp.zeros((1, 1), F32)
            for p in range(n_parts):
                k_ref = kv_refs[2 * p]
                step = min(k_ref.shape[1], KEY_NORM_ROWS)

                def scan(i, best, k_ref=k_ref, step=step, hh=hh):
                    k = k_ref[0, pl.ds(pl.multiple_of(i * step, step), step), qk_cols(hh)].astype(F32)
                    return jnp.maximum(best, jnp.max(jnp.sum(k * k, axis=-1, keepdims=True),
                                                     axis=0, keepdims=True))

                best = lax.fori_loop(0, k_ref.shape[1] // step, scan, best)
            kmax_s[hh] = jnp.broadcast_to(jnp.sqrt(best), kmax_s.shape[1:])

    def rows(r):
        return slice(r * sub, (r + 1) * sub)

    def scores(unit):
        hh, r = unit
        q = q_ref[0, rows(r), qk_cols(hh)]
        return [_dot_nt(q, kv_refs[2 * p][0, :, qk_cols(hh)]) for p in range(n_parts)]

    def finish(unit, ss, m):
        hh, r = unit
        o2 = functools.reduce(jnp.add, [_dot(jnp.exp2(s - m).astype(BF16),
                                             kv_refs[2 * p + 1][0, :, qk_cols(hh)])
                                        for p, s in enumerate(ss)])
        o_ref[0, rows(r), out_cols(hh)] = (o2[:, :HEAD] / o2[:, HEAD:]
                                           * _silu(z_ref[0, rows(r), out_cols(hh)])).astype(BF16)
        return o2[:, HEAD:]

    units = [(hh, r) for hh in range(heads) for r in range(n_sub)]
    pending = [scores(u) for u in units[:ATTN_AHEAD]]
    smallest = None
    for x, unit in enumerate(units):
        if x + ATTN_AHEAD < len(units):
            pending.append(scores(units[x + ATTN_AHEAD]))
        hh, r = unit
        q = q_ref[0, rows(r), qk_cols(hh)].astype(F32)
        bound = jnp.sqrt(jnp.sum(q * q, axis=-1, keepdims=True)) * kmax_s[hh, 0:1, 0:1]
        den = finish(unit, pending.pop(0), bound)
        smallest = den if smallest is None else jnp.minimum(smallest, den)

    @pl.when(jnp.min(smallest) < MIN_ROW_SUM)
    def _():
        for unit in units:
            ss = scores(unit)
            finish(unit, ss, functools.reduce(jnp.maximum,
                                              [jnp.max(s, axis=-1, keepdims=True) for s in ss]))


def _attn_call(q, kv_parts, proj):
    b, l, _ = q.shape
    tq = _tile(l, 1024)
    heads = H_B if l <= Q_SUB else 1
    in_specs = [pl.BlockSpec((1, tq, heads * QK_PAD), lambda i, h, j: (i, j, h))]
    args = [q]
    for k, v in kv_parts:
        lk = k.shape[1]
        in_specs += [pl.BlockSpec((1, lk, heads * QK_PAD), lambda i, h, j: (i, 0, h)),
                     pl.BlockSpec((1, lk, heads * 2 * HEAD), lambda i, h, j: (i, 0, h))]
        args += [k, v]
    in_specs.append(pl.BlockSpec((1, tq, heads * HEAD),
                                 lambda i, h, j: (i, j, COL_ZB // (heads * HEAD) + h)))
    args.append(proj)
    return pl.pallas_call(
        functools.partial(_attn_kernel, n_parts=len(kv_parts), heads=heads),
        out_shape=jax.ShapeDtypeStruct((b, l, W_B), BF16),
        grid=(b, H_B // heads, l // tq),
        in_specs=in_specs,
        out_specs=pl.BlockSpec((1, tq, heads * HEAD), lambda i, h, j: (i, j, h)),
        scratch_shapes=[pltpu.VMEM((heads, SUBLANES, LANES), F32)],
        compiler_params=_cparams(("parallel", "parallel", "arbitrary")),
        name="mla_attention",
    )(*args)


def _group(n, most):
    g = most
    while n % g:
        g //= 2
    return g


def _iotas(size):
    return (lax.broadcasted_iota(jnp.int32, (size, size), 0),
            lax.broadcasted_iota(jnp.int32, (size, size), 1))


def _chunk(j):
    return pl.ds(pl.multiple_of(j * CHUNK, CHUNK), CHUNK)


def _gdn_kernel(*refs, seq, nb, has_init, emit_state):
    q_ref, k_ref, v_ref, z_ref, gt_ref, cwq_ref, cwk_ref, cwv_ref, par_ref, norm_ref = refs[:10]
    pos = 10
    s0_ref = None
    if has_init:
        s0_ref = refs[pos]
        pos += 1
    o_ref = refs[pos]
    pos += 1
    sfin_ref = None
    if emit_state:
        sfin_ref = refs[pos]
        pos += 1
    m_s, ku_s, qu_s, egl_s, pad_s = refs[pos:pos + 5]

    n = seq // CHUNK
    head = pl.program_id(1)
    group = _group(nb * n, 8)
    r, c = _iotas(2 * CHUNK)

    def where(a):
        return (0, a) if nb == 1 else divmod(a, n)

    lane = lax.broadcasted_iota(jnp.int32, (CHUNK, LANES), 1)
    bias = par_ref[0:1, :]
    neg_a = -jnp.exp(par_ref[1:2, :])

    for which, ref in enumerate((q_ref, k_ref, v_ref)):
        for s in range(nb):
            pad_s[which, s, 0:SUBLANES, :] = jnp.zeros((SUBLANES, HEAD), F32)
            pad_s[which, s, SUBLANES + seq:, :] = jnp.zeros((SUBLANES, HEAD), F32)

            def fill(j, carry, which=which, ref=ref, s=s):
                pad_s[which, s, pl.ds(pl.multiple_of(j * CHUNK + SUBLANES, SUBLANES), CHUNK), :] = (
                    ref[s, _chunk(j), :])
                return carry

            lax.fori_loop(0, n, fill, 0, unroll=4 if n % 4 == 0 else 1)

    def conv_silu(which, w_ref, a):
        s, j = where(a)
        w = w_ref[...]
        y = None
        for t in range(5):
            tap = w[t:t + 1, :] * pad_s[which, s, pl.ds(j * CHUNK + SUBLANES - 2 + t, CHUNK), :]
            y = tap if y is None else y + tap
        return _silu(y)

    def l2n(x):
        return x * lax.rsqrt(jnp.sum(x * x, axis=-1, keepdims=True) + EPS)

    top = r < CHUNK
    same = (r >> 6) == (c >> 6)
    rd = jnp.where(top, r, -r)
    cd = jnp.where(top, c, -c)
    incl = same & (rd >= cd)
    strict = same & (rd > cd)
    incl_f = incl.astype(F32).astype(BF16)

    def pre(jj, carry):
        pairs = [jj * group + p for p in range(group)]
        qs, ks, vs, gs, betas = [], [], [], [], []
        for a in pairs:
            q = l2n(conv_silu(0, cwq_ref, a)) * QK_SCALE
            k = l2n(conv_silu(1, cwk_ref, a))
            v = conv_silu(2, cwv_ref, a)
            qs.append(jnp.concatenate([q, q], axis=0))
            ks.append(jnp.concatenate([k, k], axis=0))
            vs.append(jnp.concatenate([v, v], axis=0))
            s, j = where(a)
            x = gt_ref[s, _chunk(j), :] + bias
            act = jnp.where(lane < 8, neg_a * _softplus(x), jax.nn.sigmoid(x))
            gs.append(jnp.concatenate([_pick_lane(act, lane, head),
                                       _pick_lane(act, lane, 4 + head)], axis=0))
            betas.append(jnp.concatenate([_pick_lane(act, lane, 8 + head),
                                          _pick_lane(act, lane, 12 + head)], axis=0))
        cols = [_masked_sums(incl_f, g) for g in gs]
        rows = [jnp.transpose(col) for col in cols]
        k16 = [k.astype(BF16) for k in ks]
        kks = [_dot_nt(k, k) for k in k16]
        qks = [_dot_nt(q.astype(BF16), k) for q, k in zip(qs, k16)]
        decays = [jnp.exp(jnp.where(incl, col - row, -jnp.inf)) for col, row in zip(cols, rows)]
        ts = _tri_inv_many([jnp.where(strict, beta * kk * decay, 0.0)
                            for beta, kk, decay in zip(betas, kks, decays)], r, c)
        egs = [jnp.exp(col) for col in cols]
        uws = [_bdot(t, jnp.concatenate([v * beta, k * beta * eg], axis=1)).astype(BF16)
               for t, v, k, beta, eg in zip(ts, vs, ks, betas, egs)]
        glasts = [jnp.where(top, col[CHUNK - 1:CHUNK, :], col[CHUNK:CHUNK + 1, :]) for col in cols]
        kds = [(k * jnp.exp(gl - col)).astype(BF16) for k, gl, col in zip(ks, glasts, cols)]
        qkuw = [_dot((qk * decay).astype(BF16), uw) for qk, decay, uw in zip(qks, decays, uws)]
        for p, a in enumerate(pairs):
            qw = (qs[p] * egs[p] - qkuw[p][:, LANES:]).astype(BF16)
            egl = jnp.exp(glasts[p])
            for d in range(2):
                rs = slice(d * CHUNK, (d + 1) * CHUNK)
                kuw = _dot_tn(kds[p][rs], uws[p][rs])
                m_s[d, a, 0:HEAD, :] = kuw[:, LANES:].astype(BF16)
                m_s[d, a, HEAD:, :] = qw[rs]
                ku_s[d, a] = kuw[:, :LANES]
                qu_s[d, _chunk(a), :] = qkuw[p][rs, :LANES]
                egl_s[d, a] = egl[d * CHUNK:d * CHUNK + SUBLANES]
        return carry

    if nb == 1:
        lax.fori_loop(0, n // group, pre, 0)
    else:
        pre(0, 0)

    chains = [(s, d) for s in range(nb) for d in range(2)]

    def body(i, carry):
        cis = [s * n + (n - 1 - i if d else i) for s, d in chains]
        ms = [_dot(m_s[d, ci], st.astype(BF16)) for (s, d), ci, st in zip(chains, cis, carry)]
        for (s, d), ci, m in zip(chains, cis, ms):
            sl = _chunk(ci)
            qu_s[d, sl, :] = m[HEAD:] + qu_s[d, sl, :]
        return tuple(st * egl_s[d, ci][0:1, :] - m[:HEAD] + ku_s[d, ci]
                     for (s, d), ci, st, m in zip(chains, cis, carry, ms))

    if has_init:
        init = tuple(s0_ref[s, d, 0] for s, d in chains)
    else:
        init = tuple(jnp.zeros((HEAD, HEAD), F32) for _ in chains)
    fin = lax.fori_loop(0, n, body, init)
    if emit_state:
        for (s, d), st in zip(chains, fin):
            sfin_ref[s, d, 0] = st

    for s in range(nb):
        def post(j, carry, s=s):
            sl = _chunk(s * n + j)
            o = _rms(qu_s[0, sl, :] + qu_s[1, sl, :], norm_ref[...])
            o_ref[s, _chunk(j), :] = (o * _silu(z_ref[s, _chunk(j), :])).astype(BF16)
            return carry

        lax.fori_loop(0, n, post, 0, unroll=2)


def _seqs_per_step(b, n):
    nb = 1
    while 2 * nb * n <= 8 and b % (2 * nb) == 0:
        nb *= 2
    return nb


def _gdn_call(proj, conv_w, par, norm, s0, emit_state):
    b, l, _ = proj.shape
    n = l // CHUNK
    nb = _seqs_per_step(b, n)
    base = COL_GDN // HEAD

    def col(k):
        return pl.BlockSpec((nb, l, HEAD), lambda i, h: (i, 0, base + k * H_A + h))

    in_specs = [col(0), col(1), col(2), col(3),
                pl.BlockSpec((nb, l, LANES), lambda i, h: (i, 0, COL_GATES // LANES)),
                pl.BlockSpec((5, HEAD), lambda i, h: (0, h)),
                pl.BlockSpec((5, HEAD), lambda i, h: (0, H_A + h)),
                pl.BlockSpec((5, HEAD), lambda i, h: (0, 2 * H_A + h)),
                pl.BlockSpec((SUBLANES, LANES), lambda i, h: (0, 0)),
                pl.BlockSpec((1, HEAD), lambda i, h: (0, 0))]
    args = [proj, proj, proj, proj, proj, conv_w, conv_w, conv_w, par, norm]
    state_spec = pl.BlockSpec((nb, 2, 1, HEAD, HEAD), lambda i, h: (i, 0, h, 0, 0))
    if s0 is not None:
        in_specs.append(state_spec)
        args.append(s0)
    out_shape = [jax.ShapeDtypeStruct((b, l, W_A), BF16)]
    out_specs = [pl.BlockSpec((nb, l, HEAD), lambda i, h: (i, 0, h))]
    if emit_state:
        out_shape.append(jax.ShapeDtypeStruct((b, 2, H_A, HEAD, HEAD), F32))
        out_specs.append(state_spec)
    return pl.pallas_call(
        functools.partial(_gdn_kernel, seq=l, nb=nb, has_init=s0 is not None, emit_state=emit_state),
        out_shape=out_shape,
        grid=(b // nb, H_A),
        in_specs=in_specs,
        out_specs=out_specs,
        scratch_shapes=[pltpu.VMEM((2, nb * n, HEAD + CHUNK, HEAD), BF16),
                        pltpu.VMEM((2, nb * n, HEAD, HEAD), F32),
                        pltpu.VMEM((2, nb * l, HEAD), F32),
                        pltpu.VMEM((2, nb * n, SUBLANES, LANES), F32),
                        pltpu.VMEM((3, nb, l + 2 * SUBLANES, HEAD), F32)],
        compiler_params=_cparams(("parallel", "parallel")),
        name="gdn",
    )(*args)


def _mlstm_kernel(*refs, seq, nb, has_init, emit_state):
    q_ref, k_ref, v_ref, og_ref, z_ref, gt_ref, par_ref, norm_ref = refs[:8]
    pos = 8
    if has_init:
        c0_ref, n0_ref, m0_ref = refs[pos:pos + 3]
        pos += 3
    o_ref = refs[pos]
    pos += 1
    if emit_state:
        cfin_ref, nfin_ref, mfin_ref = refs[pos:pos + 3]
        pos += 3
    bc_s, mx_s, lw_s, ld_s, qk_s, bl_s, h_s = refs[pos:pos + 7]

    n = seq // CHUNK
    head = pl.program_id(1)
    group = _group(nb * n, 8)
    r, c = _iotas(CHUNK)

    def where(a):
        return (0, a) if nb == 1 else divmod(a, n)

    lane = lax.broadcasted_iota(jnp.int32, (CHUNK, LANES), 1)
    bias = par_ref[0:1, :]
    low = (r >= c).astype(F32).astype(BF16)
    ones = jnp.ones((CHUNK, LANES), F32)

    def dup(x):
        return jnp.concatenate([x, x], axis=1)

    def pre(jj, carry):
        chunks = [jj * group + p for p in range(group)]
        ipss, lfss = [], []
        for a in chunks:
            s, j = where(a)
            x = gt_ref[s, _chunk(j), :] + bias
            act = jnp.where(lane < 24, x, -_softplus(-x))
            ipss.append([_pick_lane(act, lane, 16 + 4 * d + head) for d in range(2)])
            lfss.append([_pick_lane(act, lane, 24 + 4 * d + head) for d in range(2)])
        prefs = [_masked_sums(low, jnp.concatenate(lfs, axis=1)) for lfs in lfss]
        for a, ips, lfs, pref in zip(chunks, ipss, lfss, prefs):
            s, j = where(a)
            sl = _chunk(a)
            qk_s[sl, :] = _dot_nt((q_ref[s, _chunk(j), :] * QK_SCALE).astype(BF16),
                                  k_ref[s, _chunk(j), :].astype(BF16)).astype(BF16)
            pref_b = pref[:, LANES:]
            bcs = (pref[:, :LANES], pref_b[CHUNK - 1:CHUNK, :] - pref_b + lfs[1])
            for d in range(2):
                incl = (r <= c) if d else (r >= c)
                bc = bcs[d]
                ip = ips[d]
                at_col = jnp.transpose(ip - bc)[:CHUNK, :]
                logd = jnp.where(incl, bc[:, :CHUNK] + at_col, -jnp.inf)
                blast = bc[0:1, :] if d else bc[CHUNK - 1:CHUNK, :]
                logw = blast - bc + ip
                bc_s[d, sl, :] = bc
                mx_s[d, sl, :] = jnp.broadcast_to(jnp.max(logd, axis=-1, keepdims=True), (CHUNK, LANES))
                lw_s[d, sl, :] = logw
                ld_s[d, sl, :] = logd
                bl_s[d, a, 0:SUBLANES, :] = jnp.broadcast_to(blast, (SUBLANES, LANES))
                bl_s[d, a, SUBLANES:, :] = jnp.broadcast_to(jnp.max(logw, axis=0, keepdims=True),
                                                            (SUBLANES, LANES))
        return carry

    if nb == 1:
        lax.fori_loop(0, n // group, pre, 0)
    else:
        pre(0, 0)

    chains = [(s, d) for s in range(nb) for d in range(2)]

    def body(i, carry):
        js = [n - 1 - i if d else i for s, d in chains]
        v2s = [jnp.concatenate([v_ref[s, _chunk(j), :], ones], axis=1).astype(BF16)
               for (s, d), j in zip(chains, js)]
        upd, decs, m_news = [], [], []
        for x, ((s, d), j) in enumerate(zip(chains, js)):
            m = carry[2 * x + 1]
            a = s * n + j
            blast = bl_s[d, a, 0:1, :]
            m_new = jnp.maximum(blast + m, bl_s[d, a, SUBLANES:SUBLANES + 1, :])
            wk = (k_ref[s, _chunk(j), :] * jnp.exp(lw_s[d, _chunk(a), :] - m_new)).astype(BF16)
            upd.append(_dot_tn(wk, v2s[x]))
            decs.append(jnp.exp(blast + m - m_new))
            m_news.append(m_new)
        for x, ((s, d), j) in enumerate(zip(chains, js)):
            cn, m = carry[2 * x], carry[2 * x + 1]
            sl = _chunk(s * n + j)
            q = (q_ref[s, _chunk(j), :] * QK_SCALE).astype(BF16)
            m_inter = bc_s[d, sl, :] + m
            m_t = jnp.maximum(m_inter, mx_s[d, sl, :])
            w_inter = jnp.exp(m_inter - m_t)
            sw = qk_s[sl, :].astype(F32) * jnp.exp(ld_s[d, sl, :] - m_t[:, :CHUNK])
            num2 = dup(w_inter) * _dot(q, cn.astype(BF16)) + _dot(sw.astype(BF16), v2s[x])
            h_s[d, sl, :] = num2[:, :LANES] / jnp.maximum(jnp.abs(num2[:, LANES:]), jnp.exp(-m_t))
        out = []
        for x in range(len(chains)):
            out += [dup(decs[x]) * carry[2 * x] + upd[x], m_news[x]]
        return tuple(out)

    init = []
    for s, d in chains:
        if has_init:
            n_b = jnp.broadcast_to(n0_ref[s, d, 0], (HEAD, LANES))
            init += [jnp.concatenate([c0_ref[s, d, 0], n_b], axis=1), m0_ref[s, d, 0]]
        else:
            init += [jnp.zeros((HEAD, 2 * LANES), F32), jnp.zeros((1, LANES), F32)]
    fin = lax.fori_loop(0, n, body, tuple(init), unroll=4)
    if emit_state:
        for x, (s, d) in enumerate(chains):
            cn, m = fin[2 * x], fin[2 * x + 1]
            cfin_ref[s, d, 0] = cn[:, :LANES]
            nfin_ref[s, d, 0] = jnp.transpose(cn[:, LANES:])[0:1, :]
            mfin_ref[s, d, 0] = m

    for s in range(nb):
        def post(j, carry, s=s):
            sl = _chunk(s * n + j)
            so = _chunk(j)
            hh = _rms(h_s[0, sl, :] + h_s[1, sl, :], norm_ref[...])
            o_ref[s, so, :] = (hh * jax.nn.sigmoid(og_ref[s, so, :]) * _silu(z_ref[s, so, :])).astype(BF16)
            return carry

        lax.fori_loop(0, n, post, 0, unroll=2)


def _mlstm_call(proj, par, norm, init, emit_state):
    b, l, _ = proj.shape
    n = l // CHUNK
    nb = _seqs_per_step(b, n)
    base = COL_MLSTM // HEAD

    def col(k):
        return pl.BlockSpec((nb, l, HEAD), lambda i, h: (i, 0, base + k * H_C + h))

    in_specs = [col(0), col(1), col(2), col(3), col(4),
                pl.BlockSpec((nb, l, LANES), lambda i, h: (i, 0, COL_GATES // LANES)),
                pl.BlockSpec((SUBLANES, LANES), lambda i, h: (0, 0)),
                pl.BlockSpec((1, HEAD), lambda i, h: (0, 0))]
    args = [proj, proj, proj, proj, proj, proj, par, norm]
    c_spec = pl.BlockSpec((nb, 2, 1, HEAD, HEAD), lambda i, h: (i, 0, h, 0, 0))
    row_spec = pl.BlockSpec((nb, 2, 1, 1, LANES), lambda i, h: (i, 0, h, 0, 0))
    if init is not None:
        in_specs += [c_spec, pl.BlockSpec((nb, 2, 1, HEAD, 1), lambda i, h: (i, 0, h, 0, 0)), row_spec]
        args += list(init)
    out_shape = [jax.ShapeDtypeStruct((b, l, W_C), BF16)]
    out_specs = [pl.BlockSpec((nb, l, HEAD), lambda i, h: (i, 0, h))]
    if emit_state:
        out_shape += [jax.ShapeDtypeStruct((b, 2, H_C, HEAD, HEAD), F32),
                      jax.ShapeDtypeStruct((b, 2, H_C, 1, LANES), F32),
                      jax.ShapeDtypeStruct((b, 2, H_C, 1, LANES), F32)]
        out_specs += [c_spec, row_spec, row_spec]
    return pl.pallas_call(
        functools.partial(_mlstm_kernel, seq=l, nb=nb, has_init=init is not None, emit_state=emit_state),
        out_shape=out_shape,
        grid=(b // nb, H_C),
        in_specs=in_specs,
        out_specs=out_specs,
        scratch_shapes=[pltpu.VMEM((2, nb * l, LANES), F32),
                        pltpu.VMEM((2, nb * l, LANES), F32),
                        pltpu.VMEM((2, nb * l, LANES), F32),
                        pltpu.VMEM((2, nb * l, CHUNK), F32),
                        pltpu.VMEM((nb * l, CHUNK), BF16),
                        pltpu.VMEM((2, nb * n, 2 * SUBLANES, LANES), F32),
                        pltpu.VMEM((2, nb * l, HEAD), F32)],
        compiler_params=_cparams(("parallel", "parallel")),
        name="mlstm",
    )(*args)


def _rope_swap(w):
    q = AXIS_ROPE // 2
    return jnp.concatenate([-w[..., q:2 * q], w[..., 0:q], -w[..., 3 * q:4 * q], w[..., 2 * q:3 * q]], axis=-1)


def _prep_w_in(w):
    qa_za, ga, ba = w[:, 0:2048], w[:, 2048:2056], w[:, 2056:2064]
    cq, ckv, kr, zb = w[:, 2064:2576], w[:, 2576:2832], w[:, 2832:2896], w[:, 2896:3920]
    mls, ic, fc = w[:, 3920:6480], w[:, 6480:6488], w[:, 6488:6496]
    pad = jnp.zeros((w.shape[0], N_PROJ - COL_GATES - 32), w.dtype)
    return jnp.concatenate([qa_za, zb, cq, mls, ckv, kr, _rope_swap(kr), ga, ba, ic, fc, pad],
                           axis=1).astype(BF16)


def _prep_w_uq(w):
    w = w.reshape(Q_LORA, H_B, HEAD + ROPE_DIM)
    pe = w[..., HEAD:]
    return jnp.concatenate([w[..., :HEAD], pe, _rope_swap(pe)], axis=-1).reshape(Q_LORA, H_B * QK_PAD).astype(BF16)


def _rope_table(seq):
    t = jnp.arange(seq)
    row = (t // GRID_W).astype(F32)
    colp = (t % GRID_W).astype(F32)
    inv = ROPE_THETA ** (-jnp.arange(0, AXIS_ROPE, 2, dtype=F32) / AXIS_ROPE)
    ar = row[:, None] * inv
    ac = colp[:, None] * inv
    cos = jnp.concatenate([jnp.cos(ar), jnp.cos(ar), jnp.cos(ac), jnp.cos(ac)], axis=-1)
    sin = jnp.concatenate([jnp.sin(ar), jnp.sin(ar), jnp.sin(ac), jnp.sin(ac)], axis=-1)
    return jnp.concatenate([cos, sin], axis=-1)


def _gate_params(first, second, lane0_a, lane0_b, extra=None):
    par = jnp.zeros((SUBLANES, LANES), F32)
    if first is not None:
        par = par.at[0, lane0_a:lane0_a + 8].set(first.reshape(-1))
    if second is not None:
        par = par.at[0, lane0_b:lane0_b + 8].set(second.reshape(-1))
    if extra is not None:
        par = par.at[1, lane0_a:lane0_a + 8].set(extra.reshape(-1))
    return par


def kernel(x_prompt, x_sample, cache_mla_ckv, cache_mla_krope, state_gdn, state_mlstm_c, state_mlstm_n,
           state_mlstm_m, c, c_ctx, w_ada, b_ada, g_pre, g_post, w_in, gdn_conv, gdn_a_log, gdn_dt_bias,
           gdn_norm, mla_q_norm, mla_kv_norm, mla_w_uq, mla_w_ukv, mlstm_b_i, mlstm_b_f, mlstm_norm,
           w_out):
    bp, lp, _ = x_prompt.shape
    bs, ls, _ = x_sample.shape

    cond = jnp.concatenate([c, c_ctx[None, :], jnp.zeros((16 - bs - 1, D_MODEL), F32)], axis=0)
    mod = _mod_call(cond, w_ada, b_ada)
    cs_lat = _rope_table(ls)
    cs_ctx = jnp.concatenate([jnp.ones((lp, ROPE_DIM), F32), jnp.zeros((lp, ROPE_DIM), F32)], axis=-1)
    kr_cache = jnp.pad(cache_mla_krope, ((0, 0), (0, 0), (0, 0), (0, LANES - ROPE_DIM)))

    xp = x_prompt.reshape(1, bp * lp, D_MODEL)
    xs = x_sample
    new_ckv, new_kr, new_gdn, new_c, new_n, new_m = [], [], [], [], [], []
    for l in range(DEPTH):
        w_in_l = _prep_w_in(w_in[l])
        wq = _prep_w_uq(mla_w_uq[l])
        wkv = mla_w_ukv[l].astype(BF16)
        w_out_l = w_out[l].astype(BF16)
        mod_l = mod[l].reshape(16, 1, 3 * D_MODEL)
        gdn_par = _gate_params(gdn_dt_bias[l], None, 0, 8, extra=gdn_a_log[l])
        mls_par = _gate_params(mlstm_b_i[l], mlstm_b_f[l], 16, 24)
        g_pre_l = g_pre[l][None, :]
        g_post_l = g_post[l][None, :]
        gdn_norm_l = gdn_norm[l][None, :]
        mls_norm_l = mlstm_norm[l][None, :]
        q_norm_l = mla_q_norm[l][None, :]
        kv_norm_l = mla_kv_norm[l][None, :]

        proj = _inproj_call(xp, g_pre_l, mod_l, bs, w_in_l).reshape(bp, lp, N_PROJ)
        mix_a, s_gdn = _gdn_call(proj, gdn_conv[l], gdn_par, gdn_norm_l, None, True)
        q, k, v, ckvn, kr = _mla_prep_call(proj, cs_ctx, q_norm_l, kv_norm_l, wq, wkv, True)
        mix_b = _attn_call(q, [(k, v)], proj)
        mix_c, cm, nm, mm = _mlstm_call(proj, mls_par, mls_norm_l, None, True)
        xp = _outproj_call(mix_a.reshape(1, bp * lp, W_A), mix_b.reshape(1, bp * lp, W_B),
                           mix_c.reshape(1, bp * lp, W_C), w_out_l, xp, mod_l, bs, g_post_l)
        new_ckv.append(ckvn)
        new_kr.append(kr)
        new_gdn.append(s_gdn)
        new_c.append(cm)
        new_n.append(nm[:, :, :, 0, :])
        new_m.append(mm[:, :, :, 0, 0])

        proj = _inproj_call(xs, g_pre_l, mod_l, 0, w_in_l)
        (mix_a,) = _gdn_call(proj, gdn_conv[l], gdn_par, gdn_norm_l, state_gdn[:, l], False)
        q, k, v = _mla_prep_call(proj, cs_lat, q_norm_l, kv_norm_l, wq, wkv, False)
        k_ctx, v_ctx = _kv_cache_call(cache_mla_ckv[:, l], kr_cache[:, l], wkv)
        mix_b = _attn_call(q, [(k_ctx, v_ctx), (k, v)], proj)
        m0 = jnp.broadcast_to(state_mlstm_m[:, l, :, :, None, None], (bs, 2, H_C, 1, LANES))
        (mix_c,) = _mlstm_call(proj, mls_par, mls_norm_l,
                               (state_mlstm_c[:, l], state_mlstm_n[:, l][..., None], m0), False)
        xs = _outproj_call(mix_a, mix_b, mix_c, w_out_l, xs, mod_l, 0, g_post_l)

    return (xp.reshape(bp, lp, D_MODEL), xs,
            jnp.stack(new_ckv, axis=1), jnp.stack(new_kr, axis=1), jnp.stack(new_gdn, axis=1),
            jnp.stack(new_c, axis=1), jnp.stack(new_n, axis=1), jnp.stack(new_m, axis=1))
```
